```python
import math
import jax, jax.numpy as jnp
from jax import lax
import numpy as np

D_MODEL = 1024
BATCH = 8
SEQ = 2048
DEPTH = 2

N_MEM = 256
D_MIX = D_MODEL
D_ATTN = D_MIX // 2
D_CONV = D_MIX - D_ATTN
HEAD_DIM = 64
N_ATTN_HEADS = D_ATTN // HEAD_DIM
CONV_WIDTH = 31
Q_BLOCK = 128
N_XATTN_HEADS = 4
XATTN_HEAD_DIM = D_MODEL // N_XATTN_HEADS
D_FF = 2816
D_IN = 3 * D_ATTN + N_ATTN_HEADS + 2 * D_CONV
EPS = 1e-6
NEG_INF = -1e30

kernel_name = "fox_conformer_macaron_hybrid"


def rmsnorm(x, g):
    xf = x.astype(jnp.float32)
    y = xf * lax.rsqrt(jnp.mean(xf * xf, axis=-1, keepdims=True) + EPS)
    return (y * g.astype(jnp.float32)).astype(x.dtype)


def layernorm(x, g, b):
    xf = x.astype(jnp.float32)
    mu = jnp.mean(xf, axis=-1, keepdims=True)
    xc = xf - mu
    y = xc * lax.rsqrt(jnp.mean(xc * xc, axis=-1, keepdims=True) + EPS)
    return (y * g.astype(jnp.float32) + b.astype(jnp.float32)).astype(x.dtype)


def swiglu(h, w_gate, w_up, w_down):
    return (jax.nn.silu(h @ w_gate) * (h @ w_up)) @ w_down


def fox_attention(q, k, v, logf):
    B, S, H, Dh = q.shape
    scale = 1.0 / math.sqrt(Dh)
    c = jnp.transpose(jnp.cumsum(logf, axis=1), (0, 2, 1))
    outs = []
    for i in range(S // Q_BLOCK):
        q0, q1 = i * Q_BLOCK, (i + 1) * Q_BLOCK
        qb = q[:, q0:q1]
        kb = k[:, :q1]
        vb = v[:, :q1]
        s = jnp.einsum('bqhd,bkhd->bhqk', qb, kb).astype(jnp.float32) * scale
        s = s + c[:, :, q0:q1, None] - c[:, :, None, :q1]
        mask = (q0 + jnp.arange(Q_BLOCK))[:, None] >= jnp.arange(q1)[None, :]
        s = jnp.where(mask[None, None], s, NEG_INF)
        p = jax.nn.softmax(s, axis=-1).astype(v.dtype)
        outs.append(jnp.einsum('bhqk,bkhd->bqhd', p, vb))
    return jnp.concatenate(outs, axis=1)


def causal_depthwise_conv(u, w, b):
    C = u.shape[-1]
    kern = w.astype(u.dtype)[:, None, :]
    y = lax.conv_general_dilated(
        u, kern, window_strides=(1,), padding=[(CONV_WIDTH - 1, 0)],
        dimension_numbers=('NWC', 'WIO', 'NWC'), feature_group_count=C)
    return y + b.astype(u.dtype)


def hybrid_mix(h, w_in, b_f, conv_w, conv_b, ln_g, ln_b, attn_g, conv_g, w_out):
    B, S, _ = h.shape
    proj = h @ w_in
    splits = [D_ATTN, 2 * D_ATTN, 3 * D_ATTN, 3 * D_ATTN + N_ATTN_HEADS,
              3 * D_ATTN + N_ATTN_HEADS + D_CONV]
    q, k, v, f_logit, a, g = jnp.split(proj, splits, axis=-1)
    q = q.reshape(B, S, N_ATTN_HEADS, HEAD_DIM)
    k = k.reshape(B, S, N_ATTN_HEADS, HEAD_DIM)
    v = v.reshape(B, S, N_ATTN_HEADS, HEAD_DIM)
    logf = jax.nn.log_sigmoid((f_logit + b_f).astype(jnp.float32))
    attn = fox_attention(q, k, v, logf).reshape(B, S, D_ATTN)
    u = a * jax.nn.sigmoid(g)
    u = causal_depthwise_conv(u, conv_w, conv_b)
    u = jax.nn.silu(layernorm(u, ln_g, ln_b))
    y = jnp.concatenate([rmsnorm(attn, attn_g), rmsnorm(u, conv_g)], axis=-1)
    return y @ w_out


def memory_cross_attention(h, mem_n, w_q, w_kv, w_o):
    B, S, _ = h.shape
    q = (h @ w_q).reshape(B, S, N_XATTN_HEADS, XATTN_HEAD_DIM)
    kv = mem_n @ w_kv
    k, v = jnp.split(kv, 2, axis=-1)
    k = k.reshape(B, -1, N_XATTN_HEADS, XATTN_HEAD_DIM)
    v = v.reshape(B, -1, N_XATTN_HEADS, XATTN_HEAD_DIM)
    s = jnp.einsum('bqhd,bmhd->bhqm', q, k).astype(jnp.float32) / math.sqrt(XATTN_HEAD_DIM)
    p = jax.nn.softmax(s, axis=-1).astype(v.dtype)
    o = jnp.einsum('bhqm,bmhd->bqhd', p, v).reshape(B, S, D_MODEL)
    return o @ w_o


def setup_inputs(seed: int = 0) -> dict:
    key = jax.random.key(seed)
    ks = iter(jax.random.split(key, 32))
    L = DEPTH

    def w(shape, fan_in):
        return jax.random.normal(next(ks), shape, jnp.float32) * (fan_in ** -0.5)

    def gain(shape):
        return 1.0 + 0.02 * jax.random.normal(next(ks), shape, jnp.float32)

    def bias(shape):
        return 0.02 * jax.random.normal(next(ks), shape, jnp.float32)

    return {
        "x": jax.random.normal(next(ks), (BATCH, SEQ, D_MODEL), jnp.float32),
        "mem": jax.random.normal(next(ks), (BATCH, N_MEM, D_MODEL), jnp.float32),
        "ffn1_norm_g": gain((L, D_MODEL)),
        "ffn1_w_gate": w((L, D_MODEL, D_FF), D_MODEL),
        "ffn1_w_up": w((L, D_MODEL, D_FF), D_MODEL),
        "ffn1_w_down": w((L, D_FF, D_MODEL), D_FF),
        "mix_norm_g": gain((L, D_MODEL)),
        "w_in": w((L, D_MODEL, D_IN), D_MODEL),
        "b_f": jax.random.uniform(next(ks), (L, N_ATTN_HEADS), jnp.float32, 1.0, 6.0),
        "conv_w": w((L, CONV_WIDTH, D_CONV), CONV_WIDTH),
        "conv_b": bias((L, D_CONV)),
        "conv_ln_g": gain((L, D_CONV)),
        "conv_ln_b": bias((L, D_CONV)),
        "attn_out_g": gain((L, D_ATTN)),
        "conv_out_g": gain((L, D_CONV)),
        "w_out": w((L, D_MIX, D_MODEL), D_MIX),
        "xattn_norm_g": gain((L, D_MODEL)),
        "mem_norm_g": gain((L, D_MODEL)),
        "xattn_w_q": w((L, D_MODEL, D_MODEL), D_MODEL),
        "xattn_w_kv": w((L, D_MODEL, 2 * D_MODEL), D_MODEL),
        "xattn_w_o": w((L, D_MODEL, D_MODEL), D_MODEL),
        "ffn2_norm_g": gain((L, D_MODEL)),
        "ffn2_w_gate": w((L, D_MODEL, D_FF), D_MODEL),
        "ffn2_w_up": w((L, D_MODEL, D_FF), D_MODEL),
        "ffn2_w_down": w((L, D_FF, D_MODEL), D_FF),
        "final_norm_g": gain((D_MODEL,)),
    }


def reference(x, mem, ffn1_norm_g, ffn1_w_gate, ffn1_w_up, ffn1_w_down, mix_norm_g, w_in, b_f,
              conv_w, conv_b, conv_ln_g, conv_ln_b, attn_out_g, conv_out_g, w_out,
              xattn_norm_g, mem_norm_g, xattn_w_q, xattn_w_kv, xattn_w_o,
              ffn2_norm_g, ffn2_w_gate, ffn2_w_up, ffn2_w_down, final_norm_g):
    for l in range(DEPTH):
        x = x + 0.5 * swiglu(rmsnorm(x, ffn1_norm_g[l]), ffn1_w_gate[l], ffn1_w_up[l], ffn1_w_down[l])
        x = x + hybrid_mix(rmsnorm(x, mix_norm_g[l]), w_in[l], b_f[l], conv_w[l], conv_b[l],
                           conv_ln_g[l], conv_ln_b[l], attn_out_g[l], conv_out_g[l], w_out[l])
        x = x + memory_cross_attention(rmsnorm(x, xattn_norm_g[l]), rmsnorm(mem, mem_norm_g[l]),
                                       xattn_w_q[l], xattn_w_kv[l], xattn_w_o[l])
        x = x + 0.5 * swiglu(rmsnorm(x, ffn2_norm_g[l]), ffn2_w_gate[l], ffn2_w_up[l], ffn2_w_down[l])
    return rmsnorm(x, final_norm_g)
```

```python
import functools
import math

import jax
import jax.numpy as jnp
from jax import lax
from jax.experimental import pallas as pl
from jax.experimental.pallas import tpu as pltpu

F32 = jnp.float32
BF16 = jnp.bfloat16

D_MODEL = 1024
N_MEM = 256
D_ATTN = 512
D_CONV = 512
HEAD_DIM = 64
N_ATTN_HEADS = 8
CONV_WIDTH = 31
N_XATTN_HEADS = 4
XATTN_HEAD_DIM = 256
D_FF = 2816
EPS = 1e-6
NEG_INF = -1e30

LANES = 128
SUBLANES = 8
MXU_DIM = 256
VMEM_LIMIT = 56 * 1024 * 1024

TOK_TILE = 512
ATT_TILE = 256
FF_CHUNK = MXU_DIM
CONV_HALO = 32
CONV_ROWS = 64


def _rms(x, g):
    return x * lax.rsqrt(jnp.mean(x * x, axis=-1, keepdims=True) + EPS) * g


def _const_spec(shape):
    nd = len(shape)
    return pl.BlockSpec(shape, lambda *_: (0,) * nd, pipeline_mode=pl.Buffered(1))


def _ffn_kernel(x_ref, g_ref, wg_ref, wu_ref, wd_ref, fg_ref, o_ref, h_scr, act_scr, *, final):
    x = x_ref[...]
    h_scr[...] = _rms(x, g_ref[...]).astype(BF16)
    for c in range(D_FF // FF_CHUNK):
        cols = slice(c * FF_CHUNK, (c + 1) * FF_CHUNK)
        h = h_scr[...]
        gate = jnp.dot(h, wg_ref[:, cols], preferred_element_type=F32)
        up = jnp.dot(h, wu_ref[:, cols], preferred_element_type=F32)
        act_scr[:, cols] = (gate * jax.nn.sigmoid(gate) * up).astype(BF16)
    y = jnp.dot(act_scr[...], wd_ref[...], preferred_element_type=F32)
    out = x + 0.5 * y
    if final:
        out = _rms(out, fg_ref[...])
    o_ref[...] = out


def _ffn(x2d, g, wg, wu, wd, fg, *, final):
    t = x2d.shape[0]
    tile = pl.BlockSpec((TOK_TILE, D_MODEL), lambda i: (i, 0))
    return pl.pallas_call(
        functools.partial(_ffn_kernel, final=final),
        out_shape=jax.ShapeDtypeStruct((t, D_MODEL), F32),
        grid=(t // TOK_TILE,),
        in_specs=[tile, _const_spec((1, D_MODEL)), _const_spec((D_MODEL, D_FF)),
                  _const_spec((D_MODEL, D_FF)), _const_spec((D_FF, D_MODEL)),
                  _const_spec((1, D_MODEL))],
        out_specs=tile,
        scratch_shapes=[pltpu.VMEM((TOK_TILE, D_MODEL), BF16), pltpu.VMEM((TOK_TILE, D_FF), BF16)],
        compiler_params=pltpu.CompilerParams(
            dimension_semantics=("arbitrary",), vmem_limit_bytes=VMEM_LIMIT),
        name="ffn_final" if final else "ffn",
    )(x2d, g, wg, wu, wd, fg)


def _memkv_kernel(mem_ref, g_ref, w_ref, k_ref, v_ref):
    mn = _rms(mem_ref[0], g_ref[0]).astype(BF16)
    kv = jnp.dot(mn, w_ref[0], preferred_element_type=F32)
    k_ref[0, 0] = kv[:, :D_MODEL].astype(BF16)
    v_ref[0, 0] = kv[:, D_MODEL:].astype(BF16)


def _memkv(mem, g, w_kv):
    depth, batch = g.shape[0], mem.shape[0]
    out = jax.ShapeDtypeStruct((depth, batch, N_MEM, D_MODEL), BF16)
    out_spec = pl.BlockSpec((1, 1, N_MEM, D_MODEL), lambda l, b: (l, b, 0, 0))
    return pl.pallas_call(
        _memkv_kernel,
        out_shape=(out, out),
        grid=(depth, batch),
        in_specs=[pl.BlockSpec((1, N_MEM, D_MODEL), lambda l, b: (b, 0, 0)),
                  pl.BlockSpec((1, 1, D_MODEL), lambda l, b: (l, 0, 0)),
                  pl.BlockSpec((1, D_MODEL, 2 * D_MODEL), lambda l, b: (l, 0, 0))],
        out_specs=(out_spec, out_spec),
        compiler_params=pltpu.CompilerParams(
            dimension_semantics=("arbitrary", "arbitrary"), vmem_limit_bytes=VMEM_LIMIT),
        name="mem_kv",
    )(mem, g, w_kv)


def _log_sigmoid(z):
    return jnp.minimum(z, 0.0) - jnp.log1p(jnp.exp(-jnp.abs(z)))


def _split3(a):
    hi = a.astype(BF16)
    r = a - hi.astype(F32)
    mid = r.astype(BF16)
    lo = (r - mid.astype(F32)).astype(BF16)
    return hi, mid, lo


def _mix_in_kernel(x_ref, g_ref, wqkv_ref, wf_ref, bf_ref, wag_ref, cw_ref, cb_ref, lng_ref,
                   lnb_ref, cog_ref, q_ref, k_ref, v_ref, c_ref, u_ref,
                   h_scr, ubuf, shifted, carry):
    s_idx = pl.program_id(1)
    h_scr[...] = _rms(x_ref[0], g_ref[...]).astype(BF16)

    qkv = jnp.dot(h_scr[...], wqkv_ref[...], preferred_element_type=F32)
    q_ref[0] = (qkv[:, :D_ATTN] * (1.0 / math.sqrt(HEAD_DIM))).astype(BF16)
    k_ref[0] = qkv[:, D_ATTN:2 * D_ATTN].astype(BF16)
    v_ref[0] = qkv[:, 2 * D_ATTN:].astype(BF16)

    @pl.when(s_idx == 0)
    def _():
        carry[...] = jnp.zeros_like(carry)
        ubuf[0:CONV_HALO, :] = jnp.zeros((CONV_HALO, D_CONV), F32)

    logf = _log_sigmoid(jnp.dot(h_scr[...], wf_ref[...], preferred_element_type=F32) + bf_ref[...])
    row = lax.broadcasted_iota(jnp.int32, (TOK_TILE, TOK_TILE), 0)
    col = lax.broadcasted_iota(jnp.int32, (TOK_TILE, TOK_TILE), 1)
    tri = (row >= col).astype(BF16)
    c_tile = carry[...]
    for part in _split3(logf):
        c_tile = c_tile + jnp.dot(tri, part, preferred_element_type=F32)
    c_ref[0] = c_tile
    carry[...] = c_tile[TOK_TILE - 1:TOK_TILE, :]

    ag = jnp.dot(h_scr[...], wag_ref[...], preferred_element_type=F32)
    ubuf[CONV_HALO:, :] = ag[:, :D_CONV] * jax.nn.sigmoid(ag[:, D_CONV:])
    n_shift = TOK_TILE + CONV_HALO - SUBLANES
    for r in range(1, SUBLANES):
        shifted[r - 1] = ubuf[r:r + n_shift, :]

    def conv_rows(rb, _):
        r0 = pl.multiple_of(rb * CONV_ROWS, CONV_ROWS)
        acc = jnp.zeros((CONV_ROWS // SUBLANES, SUBLANES, D_CONV), F32)
        for tap in range(CONV_WIDTH):
            off = tap + CONV_HALO - (CONV_WIDTH - 1)
            sh, al = off % SUBLANES, (off // SUBLANES) * SUBLANES
            if sh == 0:
                blk = ubuf[pl.ds(r0 + al, CONV_ROWS), :]
            else:
                blk = shifted[sh - 1, pl.ds(r0 + al, CONV_ROWS), :]
            acc = acc + blk.reshape(CONV_ROWS // SUBLANES, SUBLANES, D_CONV) * cw_ref[tap]
        y = acc.reshape(CONV_ROWS, D_CONV) + cb_ref[...]
        mu = jnp.mean(y, axis=-1, keepdims=True)
        yc = y - mu
        yn = yc * lax.rsqrt(jnp.mean(yc * yc, axis=-1, keepdims=True) + EPS) * lng_ref[...] + lnb_ref[...]
        sw = yn * jax.nn.sigmoid(yn)
        u_ref[0, pl.ds(r0, CONV_ROWS), :] = _rms(sw, cog_ref[...]).astype(BF16)
        return 0

    lax.fori_loop(0, TOK_TILE // CONV_ROWS, conv_rows, 0)
    ubuf[0:CONV_HALO, :] = ubuf[TOK_TILE:TOK_TILE + CONV_HALO, :]


def _mix_in(x, g, wqkv, wf, bf, wag, cw, cb, lng, lnb, cog):
    batch, seq, _ = x.shape
    tok = lambda w: pl.BlockSpec((1, TOK_TILE, w), lambda b, s: (b, s, 0))
    n_shift = TOK_TILE + CONV_HALO - SUBLANES
    outs = (jax.ShapeDtypeStruct((batch, seq, D_ATTN), BF16),) * 3 + (
        jax.ShapeDtypeStruct((batch, seq, LANES), F32),
        jax.ShapeDtypeStruct((batch, seq, D_CONV), BF16))
    return pl.pallas_call(
        _mix_in_kernel,
        out_shape=outs,
        grid=(batch, seq // TOK_TILE),
        in_specs=[tok(D_MODEL), _const_spec((1, D_MODEL)), _const_spec((D_MODEL, 3 * D_ATTN)),
                  _const_spec((D_MODEL, LANES)), _const_spec((1, LANES)),
                  _const_spec((D_MODEL, 2 * D_CONV)), _const_spec((CONV_WIDTH, SUBLANES, D_CONV)),
                  _const_spec((1, D_CONV)), _const_spec((1, D_CONV)), _const_spec((1, D_CONV)),
                  _const_spec((1, D_CONV))],
        out_specs=(tok(D_ATTN), tok(D_ATTN), tok(D_ATTN), tok(LANES), tok(D_CONV)),
        scratch_shapes=[pltpu.VMEM((TOK_TILE, D_MODEL), BF16),
                        pltpu.VMEM((TOK_TILE + CONV_HALO, D_CONV), F32),
                        pltpu.VMEM((SUBLANES - 1, n_shift, D_CONV), F32),
                        pltpu.VMEM((1, LANES), F32)],
        compiler_params=pltpu.CompilerParams(
            dimension_semantics=("arbitrary", "arbitrary"), vmem_limit_bytes=VMEM_LIMIT),
        name="mix_in",
    )(x, g, wqkv, wf, bf, wag, cw, cb, lng, lnb, cog)


def _fox_kernel(q_ref, k_ref, v_ref, ccol_ref, crow_ref, o_ref, m_scr, l_scr, acc_scr):
    i = pl.program_id(2)
    lane = lax.broadcasted_iota(jnp.int32, (ATT_TILE, LANES), 1)
    q = q_ref[0]
    zero = jnp.zeros_like(q)
    q_heads = (jnp.where(lane < HEAD_DIM, q, zero), jnp.where(lane >= HEAD_DIM, q, zero))
    c_t = ccol_ref[0, 0]

    m_scr[...] = jnp.full(m_scr.shape, NEG_INF, F32)
    l_scr[...] = jnp.zeros(l_scr.shape, F32)
    acc_scr[...] = jnp.zeros(acc_scr.shape, F32)

    def block(j, diagonal):
        k0 = pl.multiple_of(j * ATT_TILE, ATT_TILE)
        kb = k_ref[0, pl.ds(k0, ATT_TILE), :]
        vb = v_ref[0, pl.ds(k0, ATT_TILE), :]
        c_s = crow_ref[0, 0, j]
        for h in range(2):
            s = lax.dot_general(q_heads[h], kb, (((1,), (1,)), ((), ())), preferred_element_type=F32)
            s = s + c_t[:, h:h + 1] - c_s[h:h + 1, :]
            if diagonal:
                r = lax.broadcasted_iota(jnp.int32, (ATT_TILE, ATT_TILE), 0)
                c = lax.broadcasted_iota(jnp.int32, (ATT_TILE, ATT_TILE), 1)
                s = jnp.where(r >= c, s, NEG_INF)
            m_old = m_scr[h]
            m_new = jnp.maximum(m_old, jnp.max(s, axis=-1, keepdims=True))
            alpha = jnp.exp(m_old - m_new)
            p = jnp.exp(s - m_new)
            l_scr[h] = alpha * l_scr[h] + jnp.sum(p, axis=-1, keepdims=True)
            acc_scr[h] = alpha * acc_scr[h] + jnp.dot(p.astype(BF16), vb, preferred_element_type=F32)
            m_scr[h] = m_new

    def full_block(j, _):
        block(j, False)
        return 0

    lax.fori_loop(0, i, full_block, 0)
    block(i, True)
    o0 = acc_scr[0] / l_scr[0]
    o1 = acc_scr[1] / l_scr[1]
    o_ref[0] = jnp.where(lane < HEAD_DIM, o0, o1)


def _fox_attention(q, k, v, ccol, crow):
    batch, seq, _ = q.shape
    n_pairs = D_ATTN // LANES
    return pl.pallas_call(
        _fox_kernel,
        out_shape=jax.ShapeDtypeStruct((batch, seq, D_ATTN), F32),
        grid=(batch, n_pairs, seq // ATT_TILE),
        in_specs=[pl.BlockSpec((1, ATT_TILE, LANES), lambda b, p, i: (b, i, p)),
                  pl.BlockSpec((1, seq, LANES), lambda b, p, i: (b, 0, p)),
                  pl.BlockSpec((1, seq, LANES), lambda b, p, i: (b, 0, p)),
                  pl.BlockSpec((1, 1, ATT_TILE, 2), lambda b, p, i: (b, p, i, 0)),
                  pl.BlockSpec((1, 1, seq // ATT_TILE, 2, ATT_TILE), lambda b, p, i: (b, p, 0, 0, 0))],
        out_specs=pl.BlockSpec((1, ATT_TILE, LANES), lambda b, p, i: (b, i, p)),
        scratch_shapes=[pltpu.VMEM((2, ATT_TILE, 1), F32), pltpu.VMEM((2, ATT_TILE, 1), F32),
                        pltpu.VMEM((2, ATT_TILE, LANES), F32)],
        compiler_params=pltpu.CompilerParams(
            dimension_semantics=("arbitrary", "arbitrary", "arbitrary"), vmem_limit_bytes=VMEM_LIMIT),
        name="fox_attn",
    )(q, k, v, ccol, crow)


def _mix_out_kernel(x_ref, a_ref, u_ref, ag_ref, wout_ref, xg_ref, wq_ref, mk_ref, mv_ref, wo_ref,
                    o_ref, o_scr):
    ya = _rms(a_ref[0], ag_ref[...]).astype(BF16)
    y = jnp.dot(ya, wout_ref[:D_ATTN, :], preferred_element_type=F32)
    y = y + jnp.dot(u_ref[0], wout_ref[D_ATTN:, :], preferred_element_type=F32)
    x = x_ref[0] + y

    hq = _rms(x, xg_ref[...]).astype(BF16)
    q = jnp.dot(hq, wq_ref[...], preferred_element_type=F32) * (1.0 / math.sqrt(XATTN_HEAD_DIM))
    q = q.astype(BF16)
    for h in range(N_XATTN_HEADS):
        cols = slice(h * XATTN_HEAD_DIM, (h + 1) * XATTN_HEAD_DIM)
        s = lax.dot_general(q[:, cols], mk_ref[0, :, cols], (((1,), (1,)), ((), ())),
                            preferred_element_type=F32)
        p = jnp.exp(s - jnp.max(s, axis=-1, keepdims=True))
        p = p / jnp.sum(p, axis=-1, keepdims=True)
        o_scr[:, cols] = jnp.dot(p.astype(BF16), mv_ref[0, :, cols],
                                 preferred_element_type=F32).astype(BF16)
    o_ref[0] = x + jnp.dot(o_scr[...], wo_ref[...], preferred_element_type=F32)


def _mix_out(x, attn, u, ag, wout, xg, wq, mk, mv, wo):
    batch, seq, _ = x.shape
    tok = lambda w: pl.BlockSpec((1, TOK_TILE, w), lambda b, s: (b, s, 0))
    mem_spec = pl.BlockSpec((1, N_MEM, D_MODEL), lambda b, s: (b, 0, 0))
    return pl.pallas_call(
        _mix_out_kernel,
        out_shape=jax.ShapeDtypeStruct((batch, seq, D_MODEL), F32),
        grid=(batch, seq // TOK_TILE),
        in_specs=[tok(D_MODEL), tok(D_ATTN), tok(D_CONV), _const_spec((1, D_ATTN)),
                  _const_spec((D_MODEL, D_MODEL)), _const_spec((1, D_MODEL)),
                  _const_spec((D_MODEL, D_MODEL)), mem_spec, mem_spec,
                  _const_spec((D_MODEL, D_MODEL))],
        out_specs=tok(D_MODEL),
        scratch_shapes=[pltpu.VMEM((TOK_TILE, D_MODEL), BF16)],
        compiler_params=pltpu.CompilerParams(
            dimension_semantics=("arbitrary", "arbitrary"), vmem_limit_bytes=VMEM_LIMIT),
        name="mix_out",
    )(x, attn, u, ag, wout, xg, wq, mk, mv, wo)


def kernel(x, mem, ffn1_norm_g, ffn1_w_gate, ffn1_w_up, ffn1_w_down, mix_norm_g, w_in, b_f, conv_w, conv_b, conv_ln_g, conv_ln_b, attn_out_g, conv_out_g, w_out, xattn_norm_g, mem_norm_g, xattn_w_q, xattn_w_kv, xattn_w_o, ffn2_norm_g, ffn2_w_gate, ffn2_w_up, ffn2_w_down, final_norm_g):
    batch, seq, _ = x.shape
    depth = w_in.shape[0]
    n_blk = seq // ATT_TILE
    n_pairs = D_ATTN // LANES
    row = lambda v: v.reshape(1, -1)

    mem_k, mem_v = _memkv(mem, mem_norm_g.reshape(depth, 1, D_MODEL), xattn_w_kv.astype(BF16))
    fg = row(final_norm_g)

    for l in range(depth):
        x2d = _ffn(x.reshape(batch * seq, D_MODEL), row(ffn1_norm_g[l]), ffn1_w_gate[l].astype(BF16),
                   ffn1_w_up[l].astype(BF16), ffn1_w_down[l].astype(BF16), fg, final=False)
        x = x2d.reshape(batch, seq, D_MODEL)

        w = w_in[l]
        wf = jnp.pad(w[:, 3 * D_ATTN:3 * D_ATTN + N_ATTN_HEADS], ((0, 0), (0, LANES - N_ATTN_HEADS)))
        bf = jnp.pad(b_f[l], (0, LANES - N_ATTN_HEADS)).reshape(1, LANES)
        cw = jnp.broadcast_to(conv_w[l][:, None, :], (CONV_WIDTH, SUBLANES, D_CONV))
        q, k, v, c, u = _mix_in(
            x, row(mix_norm_g[l]), w[:, :3 * D_ATTN].astype(BF16), wf.astype(BF16), bf,
            w[:, 3 * D_ATTN + N_ATTN_HEADS:].astype(BF16), cw, row(conv_b[l]), row(conv_ln_g[l]),
            row(conv_ln_b[l]), row(conv_out_g[l]))

        c8 = c[:, :, :N_ATTN_HEADS]
        ccol = c8.reshape(batch, seq, n_pairs, 2).transpose(0, 2, 1, 3)
        crow = c8.reshape(batch, n_blk, ATT_TILE, n_pairs, 2).transpose(0, 3, 1, 4, 2)
        attn = _fox_attention(q, k, v, ccol, crow)

        x = _mix_out(x, attn, u, row(attn_out_g[l]), w_out[l].astype(BF16), row(xattn_norm_g[l]),
                     xattn_w_q[l].astype(BF16), mem_k[l], mem_v[l], xattn_w_o[l].astype(BF16))

        last = l == depth - 1
        x2d = _ffn(x.reshape(batch * seq, D_MODEL), row(ffn2_norm_g[l]), ffn2_w_gate[l].astype(BF16),
                   ffn2_w_up[l].astype(BF16), ffn2_w_down[l].astype(BF16), fg, final=last)
        x = x2d.reshape(batch, seq, D_MODEL)
    return x
```

```python
import functools
import math

import jax
import jax.numpy as jnp
from jax import lax
from jax.experimental import pallas as pl
from jax.experimental.pallas import tpu as pltpu

F32 = jnp.float32
BF16 = jnp.bfloat16

D_MODEL = 1024
N_MEM = 256
D_ATTN = 512
D_CONV = 512
HEAD_DIM = 64
N_ATTN_HEADS = 8
CONV_WIDTH = 31
N_XATTN_HEADS = 4
XATTN_HEAD_DIM = 256
D_FF = 2816
EPS = 1e-6
NEG_INF = -1e30
LOG2E = math.log2(math.e)

LANES = 128
SUBLANES = 8
MXU_DIM = 256
VMEM_LIMIT = 56 * 1024 * 1024

TOK_TILE = 512
ATT_TILE = 256
FF_CHUNK = MXU_DIM
CONV_HALO = 32
CONV_ROWS = 64
N_PAIRS = D_ATTN // LANES


def _rms(x, g):
    return x * lax.rsqrt(jnp.mean(x * x, axis=-1, keepdims=True) + EPS) * g


def _const_spec(shape):
    nd = len(shape)
    return pl.BlockSpec(shape, lambda *_: (0,) * nd, pipeline_mode=pl.Buffered(1))


def _ffn_kernel(x_ref, g_ref, wg_ref, wu_ref, wd_ref, fg_ref, o_ref, h_scr, act_scr, *, final):
    x = x_ref[...]
    h_scr[...] = _rms(x, g_ref[...]).astype(BF16)
    for c in range(D_FF // FF_CHUNK):
        cols = slice(c * FF_CHUNK, (c + 1) * FF_CHUNK)
        h = h_scr[...]
        gate = jnp.dot(h, wg_ref[:, cols], preferred_element_type=F32)
        up = jnp.dot(h, wu_ref[:, cols], preferred_element_type=F32)
        act_scr[:, cols] = (gate * jax.nn.sigmoid(gate) * up).astype(BF16)
    y = jnp.dot(act_scr[...], wd_ref[...], preferred_element_type=F32)
    out = x + 0.5 * y
    if final:
        out = _rms(out, fg_ref[...])
    o_ref[...] = out


def _ffn(x2d, g, wg, wu, wd, fg, *, final):
    t = x2d.shape[0]
    tile = pl.BlockSpec((TOK_TILE, D_MODEL), lambda i: (i, 0))
    return pl.pallas_call(
        functools.partial(_ffn_kernel, final=final),
        out_shape=jax.ShapeDtypeStruct((t, D_MODEL), F32),
        grid=(t // TOK_TILE,),
        in_specs=[tile, _const_spec((1, D_MODEL)), _const_spec((D_MODEL, D_FF)),
                  _const_spec((D_MODEL, D_FF)), _const_spec((D_FF, D_MODEL)),
                  _const_spec((1, D_MODEL))],
        out_specs=tile,
        scratch_shapes=[pltpu.VMEM((TOK_TILE, D_MODEL), BF16), pltpu.VMEM((TOK_TILE, D_FF), BF16)],
        compiler_params=pltpu.CompilerParams(
            dimension_semantics=("arbitrary",), vmem_limit_bytes=VMEM_LIMIT),
        name="ffn_final" if final else "ffn",
    )(x2d, g, wg, wu, wd, fg)


def _memkv_kernel(mem_ref, g_ref, w_ref, k_ref, v_ref):
    mn = _rms(mem_ref[0], g_ref[0]).astype(BF16)
    kv = jnp.dot(mn, w_ref[0], preferred_element_type=F32)
    k_ref[0, 0] = kv[:, :D_MODEL].astype(BF16)
    v_ref[0, 0] = kv[:, D_MODEL:].astype(BF16)


def _memkv(mem, g, w_kv):
    depth, batch = g.shape[0], mem.shape[0]
    out = jax.ShapeDtypeStruct((depth, batch, N_MEM, D_MODEL), BF16)
    out_spec = pl.BlockSpec((1, 1, N_MEM, D_MODEL), lambda l, b: (l, b, 0, 0))
    return pl.pallas_call(
        _memkv_kernel,
        out_shape=(out, out),
        grid=(depth, batch),
        in_specs=[pl.BlockSpec((1, N_MEM, D_MODEL), lambda l, b: (b, 0, 0)),
                  pl.BlockSpec((1, 1, D_MODEL), lambda l, b: (l, 0, 0)),
                  pl.BlockSpec((1, D_MODEL, 2 * D_MODEL), lambda l, b: (l, 0, 0))],
        out_specs=(out_spec, out_spec),
        compiler_params=pltpu.CompilerParams(
            dimension_semantics=("arbitrary", "arbitrary"), vmem_limit_bytes=VMEM_LIMIT),
        name="mem_kv",
    )(mem, g, w_kv)


def _log_sigmoid(z):
    return jnp.minimum(z, 0.0) - jnp.log1p(jnp.exp(-jnp.abs(z)))


def _split3(a):
    hi = a.astype(BF16).astype(F32)
    r = a - hi
    mid = r.astype(BF16).astype(F32)
    lo = (r - mid).astype(BF16).astype(F32)
    return hi, mid, lo


def _mix_in_kernel(x_ref, g_ref, wqkv_ref, wf_ref, bf_ref, tri_ref, wag_ref, cw_ref, cb_ref, lng_ref,
                   lnb_ref, cog_ref, q_ref, k_ref, v_ref, c_ref, u_ref,
                   h_scr, ubuf, shifted, carry):
    s_idx = pl.program_id(1)

    @pl.when(s_idx == 0)
    def _():
        carry[...] = jnp.zeros_like(carry)
        ubuf[0:CONV_HALO, :] = jnp.zeros((CONV_HALO, D_CONV), F32)

    h_scr[...] = _rms(x_ref[0], g_ref[...]).astype(BF16)

    ag = jnp.dot(h_scr[...], wag_ref[...], preferred_element_type=F32)
    ubuf[CONV_HALO:, :] = ag[:, :D_CONV] * jax.nn.sigmoid(ag[:, D_CONV:])
    n_shift = TOK_TILE + CONV_HALO - SUBLANES
    for r in range(1, SUBLANES):
        shifted[r - 1] = ubuf[r:r + n_shift, :]

    qkv = jnp.dot(h_scr[...], wqkv_ref[...], preferred_element_type=F32)
    q_ref[0] = (qkv[:, :D_ATTN] * (LOG2E / math.sqrt(HEAD_DIM))).astype(BF16)
    k_ref[0] = qkv[:, D_ATTN:2 * D_ATTN].astype(BF16)
    v_ref[0] = qkv[:, 2 * D_ATTN:].astype(BF16)

    f_t = jnp.transpose(jnp.dot(h_scr[...], wf_ref[...], preferred_element_type=F32))
    logf = _log_sigmoid(f_t[:N_ATTN_HEADS, :] + bf_ref[...]) * LOG2E
    parts = jnp.concatenate(_split3(logf) + (jnp.zeros_like(logf),), axis=0).astype(BF16)
    sums = jnp.dot(parts, tri_ref[...], preferred_element_type=F32)
    h8 = N_ATTN_HEADS
    c_tile = carry[...] + sums[:h8] + sums[h8:2 * h8] + sums[2 * h8:3 * h8]
    c_ref[0] = c_tile
    carry[...] = c_tile[:, TOK_TILE - 1:TOK_TILE]

    for rb in range(TOK_TILE // CONV_ROWS):
        r0 = rb * CONV_ROWS
        acc = jnp.zeros((CONV_ROWS // SUBLANES, SUBLANES, D_CONV), F32)
        for tap in range(CONV_WIDTH):
            off = r0 + tap + CONV_HALO - (CONV_WIDTH - 1)
            sh, al = off % SUBLANES, (off // SUBLANES) * SUBLANES
            if sh == 0:
                blk = ubuf[al:al + CONV_ROWS, :]
            else:
                blk = shifted[sh - 1, al:al + CONV_ROWS, :]
            acc = acc + blk.reshape(CONV_ROWS // SUBLANES, SUBLANES, D_CONV) * cw_ref[tap]
        y = acc.reshape(CONV_ROWS, D_CONV) + cb_ref[...]
        mu = jnp.mean(y, axis=-1, keepdims=True)
        yc = y - mu
        yn = yc * lax.rsqrt(jnp.mean(yc * yc, axis=-1, keepdims=True) + EPS) * lng_ref[...] + lnb_ref[...]
        sw = yn * jax.nn.sigmoid(yn)
        u_ref[0, r0:r0 + CONV_ROWS, :] = _rms(sw, cog_ref[...]).astype(BF16)

    ubuf[0:CONV_HALO, :] = ubuf[TOK_TILE:TOK_TILE + CONV_HALO, :]


def _mix_in(x, g, wqkv, wf, bf, tri, wag, cw, cb, lng, lnb, cog):
    batch, seq, _ = x.shape
    tok = lambda w: pl.BlockSpec((1, TOK_TILE, w), lambda b, s: (b, s, 0))
    n_shift = TOK_TILE + CONV_HALO - SUBLANES
    outs = (jax.ShapeDtypeStruct((batch, seq, D_ATTN), BF16),) * 3 + (
        jax.ShapeDtypeStruct((batch, N_ATTN_HEADS, seq), F32),
        jax.ShapeDtypeStruct((batch, seq, D_CONV), BF16))
    return pl.pallas_call(
        _mix_in_kernel,
        out_shape=outs,
        grid=(batch, seq // TOK_TILE),
        in_specs=[tok(D_MODEL), _const_spec((1, D_MODEL)), _const_spec((D_MODEL, 3 * D_ATTN)),
                  _const_spec((D_MODEL, LANES)), _const_spec((N_ATTN_HEADS, 1)),
                  _const_spec((TOK_TILE, TOK_TILE)),
                  _const_spec((D_MODEL, 2 * D_CONV)), _const_spec((CONV_WIDTH, SUBLANES, D_CONV)),
                  _const_spec((1, D_CONV)), _const_spec((1, D_CONV)), _const_spec((1, D_CONV)),
                  _const_spec((1, D_CONV))],
        out_specs=(tok(D_ATTN), tok(D_ATTN), tok(D_ATTN),
                   pl.BlockSpec((1, N_ATTN_HEADS, TOK_TILE), lambda b, s: (b, 0, s)), tok(D_CONV)),
        scratch_shapes=[pltpu.VMEM((TOK_TILE, D_MODEL), BF16),
                        pltpu.VMEM((TOK_TILE + CONV_HALO, D_CONV), F32),
                        pltpu.VMEM((SUBLANES - 1, n_shift, D_CONV), F32),
                        pltpu.VMEM((N_ATTN_HEADS, 1), F32)],
        compiler_params=pltpu.CompilerParams(
            dimension_semantics=("arbitrary", "arbitrary"), vmem_limit_bytes=VMEM_LIMIT),
        name="mix_in",
    )(x, g, wqkv, wf, bf, tri, wag, cw, cb, lng, lnb, cog)


def _fox_kernel(q_ref, k_ref, v_ref, ccol_ref, crow_ref, g_ref, o_ref,
                qm_scr, ct_scr, m_scr, l_scr, acc_scr):
    i = pl.program_id(1)
    n_chunk = ATT_TILE // LANES
    lane = lax.broadcasted_iota(jnp.int32, (ATT_TILE, LANES), 1)
    first = lane < HEAD_DIM
    row = lax.broadcasted_iota(jnp.int32, (ATT_TILE, LANES), 0)

    c_t = ccol_ref[0]
    for p in range(N_PAIRS):
        qp = q_ref[0, :, p * LANES:(p + 1) * LANES]
        zero = jnp.zeros_like(qp)
        qm_scr[2 * p] = jnp.where(first, qp, zero)
        qm_scr[2 * p + 1] = jnp.where(first, zero, qp)
    for h in range(N_ATTN_HEADS):
        ct_scr[h] = jnp.broadcast_to(c_t[:, h:h + 1], (ATT_TILE, LANES))
    m_scr[...] = jnp.full(m_scr.shape, NEG_INF, F32)
    l_scr[...] = jnp.zeros(l_scr.shape, F32)
    acc_scr[...] = jnp.zeros(acc_scr.shape, F32)

    def block(j, diagonal):
        k0 = pl.multiple_of(j * ATT_TILE, ATT_TILE)
        c_s = crow_ref[0, j]
        for p in range(N_PAIRS):
            kb = k_ref[0, pl.ds(k0, ATT_TILE), p * LANES:(p + 1) * LANES]
            vb = v_ref[0, pl.ds(k0, ATT_TILE), p * LANES:(p + 1) * LANES]
            pv, alphas = [], []
            for h in (2 * p, 2 * p + 1):
                s = lax.dot_general(qm_scr[h], kb, (((1,), (1,)), ((), ())), preferred_element_type=F32)
                ct = ct_scr[h]
                chunks = []
                for c in range(n_chunk):
                    sc = s[:, c * LANES:(c + 1) * LANES] + ct - c_s[h:h + 1, c * LANES:(c + 1) * LANES]
                    if diagonal:
                        sc = jnp.where(row >= lane + c * LANES, sc, NEG_INF)
                    chunks.append(sc)
                m_old = m_scr[h]
                m_blk = functools.reduce(jnp.maximum, chunks)
                m_new = jnp.maximum(m_old, jnp.max(m_blk, axis=-1, keepdims=True))
                alpha = jnp.exp2(m_old - m_new)
                ps = [jnp.exp2(sc - m_new) for sc in chunks]
                l_scr[h] = alpha * l_scr[h] + functools.reduce(jnp.add, ps)
                m_scr[h] = m_new
                p_bf = jnp.concatenate(ps, axis=1).astype(BF16)
                pv.append(jnp.dot(p_bf, vb, preferred_element_type=F32))
                alphas.append(alpha)
            acc_scr[p] = (jnp.where(first, alphas[0], alphas[1]) * acc_scr[p]
                          + jnp.where(first, pv[0], pv[1]))

    def full_block(j, _):
        block(j, False)
        return 0

    lax.fori_loop(0, i, full_block, 0)
    block(i, True)

    outs = []
    for p in range(N_PAIRS):
        l0 = jnp.sum(l_scr[2 * p], axis=-1, keepdims=True)
        l1 = jnp.sum(l_scr[2 * p + 1], axis=-1, keepdims=True)
        outs.append(acc_scr[p] / jnp.where(first, l0, l1))
    o_ref[0] = _rms(jnp.concatenate(outs, axis=1), g_ref[...]).astype(BF16)


def _fox_attention(q, k, v, ccol, crow, g):
    batch, seq, _ = q.shape
    n_blk = seq // ATT_TILE
    stat = pltpu.VMEM((N_ATTN_HEADS, ATT_TILE, LANES), F32)
    return pl.pallas_call(
        _fox_kernel,
        out_shape=jax.ShapeDtypeStruct((batch, seq, D_ATTN), BF16),
        grid=(batch, n_blk),
        in_specs=[pl.BlockSpec((1, ATT_TILE, D_ATTN), lambda b, i: (b, i, 0)),
                  pl.BlockSpec((1, seq, D_ATTN), lambda b, i: (b, 0, 0)),
                  pl.BlockSpec((1, seq, D_ATTN), lambda b, i: (b, 0, 0)),
                  pl.BlockSpec((1, ATT_TILE, N_ATTN_HEADS), lambda b, i: (b, i, 0)),
                  pl.BlockSpec((1, n_blk, N_ATTN_HEADS, ATT_TILE), lambda b, i: (b, 0, 0, 0)),
                  _const_spec((1, D_ATTN))],
        out_specs=pl.BlockSpec((1, ATT_TILE, D_ATTN), lambda b, i: (b, i, 0)),
        scratch_shapes=[pltpu.VMEM((N_ATTN_HEADS, ATT_TILE, LANES), BF16), stat, stat, stat,
                        pltpu.VMEM((N_PAIRS, ATT_TILE, LANES), F32)],
        compiler_params=pltpu.CompilerParams(
            dimension_semantics=("arbitrary", "arbitrary"), vmem_limit_bytes=VMEM_LIMIT),
        name="fox_attn",
    )(q, k, v, ccol, crow, g)


def _mix_out_kernel(x_ref, a_ref, u_ref, wout_ref, xg_ref, wq_ref, mk_ref, mv_ref, wo_ref,
                    o_ref, o_scr):
    y = jnp.dot(a_ref[0], wout_ref[:D_ATTN, :], preferred_element_type=F32)
    y = y + jnp.dot(u_ref[0], wout_ref[D_ATTN:, :], preferred_element_type=F32)
    x = x_ref[0] + y

    hq = _rms(x, xg_ref[...]).astype(BF16)
    q = jnp.dot(hq, wq_ref[...], preferred_element_type=F32) * (1.0 / math.sqrt(XATTN_HEAD_DIM))
    q = q.astype(BF16)
    for h in range(N_XATTN_HEADS):
        cols = slice(h * XATTN_HEAD_DIM, (h + 1) * XATTN_HEAD_DIM)
        s = lax.dot_general(q[:, cols], mk_ref[0, :, cols], (((1,), (1,)), ((), ())),
                            preferred_element_type=F32)
        p = jnp.exp(s - jnp.max(s, axis=-1, keepdims=True))
        p = p / jnp.sum(p, axis=-1, keepdims=True)
        o_scr[:, cols] = jnp.dot(p.astype(BF16), mv_ref[0, :, cols],
                                 preferred_element_type=F32).astype(BF16)
    o_ref[0] = x + jnp.dot(o_scr[...], wo_ref[...], preferred_element_type=F32)


def _mix_out(x, attn, u, wout, xg, wq, mk, mv, wo):
    batch, seq, _ = x.shape
    tok = lambda w: pl.BlockSpec((1, TOK_TILE, w), lambda b, s: (b, s, 0))
    mem_spec = pl.BlockSpec((1, N_MEM, D_MODEL), lambda b, s: (b, 0, 0))
    return pl.pallas_call(
        _mix_out_kernel,
        out_shape=jax.ShapeDtypeStruct((batch, seq, D_MODEL), F32),
        grid=(batch, seq // TOK_TILE),
        in_specs=[tok(D_MODEL), tok(D_ATTN), tok(D_CONV),
                  _const_spec((D_MODEL, D_MODEL)), _const_spec((1, D_MODEL)),
                  _const_spec((D_MODEL, D_MODEL)), mem_spec, mem_spec,
                  _const_spec((D_MODEL, D_MODEL))],
        out_specs=tok(D_MODEL),
        scratch_shapes=[pltpu.VMEM((TOK_TILE, D_MODEL), BF16)],
        compiler_params=pltpu.CompilerParams(
            dimension_semantics=("arbitrary", "arbitrary"), vmem_limit_bytes=VMEM_LIMIT),
        name="mix_out",
    )(x, attn, u, wout, xg, wq, mk, mv, wo)


def kernel(x, mem, ffn1_norm_g, ffn1_w_gate, ffn1_w_up, ffn1_w_down, mix_norm_g, w_in, b_f, conv_w, conv_b, conv_ln_g, conv_ln_b, attn_out_g, conv_out_g, w_out, xattn_norm_g, mem_norm_g, xattn_w_q, xattn_w_kv, xattn_w_o, ffn2_norm_g, ffn2_w_gate, ffn2_w_up, ffn2_w_down, final_norm_g):
    batch, seq, _ = x.shape
    depth = w_in.shape[0]
    n_blk = seq // ATT_TILE
    row = lambda v: v.reshape(1, -1)

    mem_k, mem_v = _memkv(mem, mem_norm_g.reshape(depth, 1, D_MODEL), xattn_w_kv.astype(BF16))
    fg = row(final_norm_g)
    tri = jnp.triu(jnp.ones((TOK_TILE, TOK_TILE), BF16))

    for l in range(depth):
        x2d = _ffn(x.reshape(batch * seq, D_MODEL), row(ffn1_norm_g[l]), ffn1_w_gate[l].astype(BF16),
                   ffn1_w_up[l].astype(BF16), ffn1_w_down[l].astype(BF16), fg, final=False)
        x = x2d.reshape(batch, seq, D_MODEL)

        w = w_in[l]
        wf = jnp.pad(w[:, 3 * D_ATTN:3 * D_ATTN + N_ATTN_HEADS], ((0, 0), (0, LANES - N_ATTN_HEADS)))
        cw = jnp.broadcast_to(conv_w[l][:, None, :], (CONV_WIDTH, SUBLANES, D_CONV))
        q, k, v, c, u = _mix_in(
            x, row(mix_norm_g[l]), w[:, :3 * D_ATTN].astype(BF16), wf.astype(BF16),
            b_f[l].reshape(N_ATTN_HEADS, 1), tri,
            w[:, 3 * D_ATTN + N_ATTN_HEADS:].astype(BF16), cw, row(conv_b[l]), row(conv_ln_g[l]),
            row(conv_ln_b[l]), row(conv_out_g[l]))

        ccol = c.transpose(0, 2, 1)
        crow = c.reshape(batch, N_ATTN_HEADS, n_blk, ATT_TILE).transpose(0, 2, 1, 3)
        attn = _fox_attention(q, k, v, ccol, crow, row(attn_out_g[l]))

        x = _mix_out(x, attn, u, w_out[l].astype(BF16), row(xattn_norm_g[l]),
                     xattn_w_q[l].astype(BF16), mem_k[l], mem_v[l], xattn_w_o[l].astype(BF16))

        last = l == depth - 1
        x2d = _ffn(x.reshape(batch * seq, D_MODEL), row(ffn2_norm_g[l]), ffn2_w_gate[l].astype(BF16),
                   ffn2_w_up[l].astype(BF16), ffn2_w_down[l].astype(BF16), fg, final=last)
        x = x2d.reshape(batch, seq, D_MODEL)
    return x
```

```python
import functools
import math

import jax
import jax.numpy as jnp
from jax import lax
from jax.experimental import pallas as pl
from jax.experimental.pallas import tpu as pltpu

F32 = jnp.float32
BF16 = jnp.bfloat16

D_MODEL = 1024
N_MEM = 256
D_ATTN = 512
D_CONV = 512
HEAD_DIM = 64
N_ATTN_HEADS = 8
CONV_WIDTH = 31
N_XATTN_HEADS = 4
XATTN_HEAD_DIM = 256
D_FF = 2816
EPS = 1e-6
NEG_INF = -1e30
LOG2E = math.log2(math.e)

LANES = 128
SUBLANES = 8
MXU_DIM = 256
VMEM_LIMIT = 56 * 1024 * 1024

TOK_TILE = 512
ATT_TILE = 256
FF_CHUNK = MXU_DIM
CONV_HALO = 32
CONV_ROWS = 64
N_PAIRS = D_ATTN // LANES


def _rms(x, g):
    return x * lax.rsqrt(jnp.mean(x * x, axis=-1, keepdims=True) + EPS) * g


def _const_spec(shape):
    nd = len(shape)
    return pl.BlockSpec(shape, lambda *_: (0,) * nd, pipeline_mode=pl.Buffered(1))


def _ffn_kernel(x_ref, g_ref, wg_ref, wu_ref, wd_ref, fg_ref, o_ref, h_scr, act_scr, *, final):
    x = x_ref[...]
    h_scr[...] = _rms(x, g_ref[...]).astype(BF16)
    for c in range(D_FF // FF_CHUNK):
        cols = slice(c * FF_CHUNK, (c + 1) * FF_CHUNK)
        h = h_scr[...]
        gate = jnp.dot(h, wg_ref[:, cols], preferred_element_type=F32)
        up = jnp.dot(h, wu_ref[:, cols], preferred_element_type=F32)
        act_scr[:, cols] = (gate * jax.nn.sigmoid(gate) * up).astype(BF16)
    y = jnp.dot(act_scr[...], wd_ref[...], preferred_element_type=F32)
    out = x + 0.5 * y
    if final:
        out = _rms(out, fg_ref[...])
    o_ref[...] = out


def _ffn(x2d, g, wg, wu, wd, fg, *, final):
    t = x2d.shape[0]
    tile = pl.BlockSpec((TOK_TILE, D_MODEL), lambda i: (i, 0))
    return pl.pallas_call(
        functools.partial(_ffn_kernel, final=final),
        out_shape=jax.ShapeDtypeStruct((t, D_MODEL), F32),
        grid=(t // TOK_TILE,),
        in_specs=[tile, _const_spec((1, D_MODEL)), _const_spec((D_MODEL, D_FF)),
                  _const_spec((D_MODEL, D_FF)), _const_spec((D_FF, D_MODEL)),
                  _const_spec((1, D_MODEL))],
        out_specs=tile,
        scratch_shapes=[pltpu.VMEM((TOK_TILE, D_MODEL), BF16), pltpu.VMEM((TOK_TILE, D_FF), BF16)],
        compiler_params=pltpu.CompilerParams(
            dimension_semantics=("arbitrary",), vmem_limit_bytes=VMEM_LIMIT),
        name="ffn_final" if final else "ffn",
    )(x2d, g, wg, wu, wd, fg)


def _memkv_kernel(mem_ref, g_ref, w_ref, k_ref, v_ref):
    mn = _rms(mem_ref[0], g_ref[0]).astype(BF16)
    kv = jnp.dot(mn, w_ref[0], preferred_element_type=F32)
    k_ref[0, 0] = kv[:, :D_MODEL].astype(BF16)
    v_ref[0, 0] = kv[:, D_MODEL:].astype(BF16)


def _memkv(mem, g, w_kv):
    depth, batch = g.shape[0], mem.shape[0]
    out = jax.ShapeDtypeStruct((depth, batch, N_MEM, D_MODEL), BF16)
    out_spec = pl.BlockSpec((1, 1, N_MEM, D_MODEL), lambda l, b: (l, b, 0, 0))
    return pl.pallas_call(
        _memkv_kernel,
        out_shape=(out, out),
        grid=(depth, batch),
        in_specs=[pl.BlockSpec((1, N_MEM, D_MODEL), lambda l, b: (b, 0, 0)),
                  pl.BlockSpec((1, 1, D_MODEL), lambda l, b: (l, 0, 0)),
                  pl.BlockSpec((1, D_MODEL, 2 * D_MODEL), lambda l, b: (l, 0, 0))],
        out_specs=(out_spec, out_spec),
        compiler_params=pltpu.CompilerParams(
            dimension_semantics=("arbitrary", "arbitrary"), vmem_limit_bytes=VMEM_LIMIT),
        name="mem_kv",
    )(mem, g, w_kv)


def _log_sigmoid(z):
    return jnp.minimum(z, 0.0) - jnp.log1p(jnp.exp(-jnp.abs(z)))


def _split3(a):
    hi = a.astype(BF16).astype(F32)
    r = a - hi
    mid = r.astype(BF16).astype(F32)
    lo = (r - mid).astype(BF16).astype(F32)
    return hi, mid, lo


def _mix_in_kernel(x_ref, g_ref, wqkv_ref, wf_ref, bf_ref, tri_ref, wag_ref, cw_ref, cb_ref, lng_ref,
                   lnb_ref, cog_ref, q_ref, k_ref, v_ref, c_ref, u_ref,
                   h_scr, ubuf, shifted, carry):
    s_idx = pl.program_id(1)

    @pl.when(s_idx == 0)
    def _():
        carry[...] = jnp.zeros_like(carry)
        ubuf[0:CONV_HALO, :] = jnp.zeros((CONV_HALO, D_CONV), F32)

    h_scr[...] = _rms(x_ref[0], g_ref[...]).astype(BF16)

    ag = jnp.dot(h_scr[...], wag_ref[...], preferred_element_type=F32)
    ubuf[CONV_HALO:, :] = ag[:, :D_CONV] * jax.nn.sigmoid(ag[:, D_CONV:])
    n_shift = TOK_TILE + CONV_HALO - SUBLANES
    for r in range(1, SUBLANES):
        shifted[r - 1] = ubuf[r:r + n_shift, :]

    qkv = jnp.dot(h_scr[...], wqkv_ref[...], preferred_element_type=F32)
    q_ref[0] = (qkv[:, :D_ATTN] * (LOG2E / math.sqrt(HEAD_DIM))).astype(BF16)
    k_ref[0] = qkv[:, D_ATTN:2 * D_ATTN].astype(BF16)
    v_ref[0] = qkv[:, 2 * D_ATTN:].astype(BF16)

    f_t = jnp.transpose(jnp.dot(h_scr[...], wf_ref[...], preferred_element_type=F32))
    logf = _log_sigmoid(f_t[:N_ATTN_HEADS, :] + bf_ref[...]) * LOG2E
    parts = jnp.concatenate(_split3(logf) + (jnp.zeros_like(logf),), axis=0).astype(BF16)
    sums = jnp.dot(parts, tri_ref[...], preferred_element_type=F32)
    h8 = N_ATTN_HEADS
    c_tile = carry[...] + sums[:h8] + sums[h8:2 * h8] + sums[2 * h8:3 * h8]
    c_ref[0] = c_tile
    carry[...] = c_tile[:, TOK_TILE - 1:TOK_TILE]

    for rb in range(TOK_TILE // CONV_ROWS):
        r0 = rb * CONV_ROWS
        acc = jnp.zeros((CONV_ROWS // SUBLANES, SUBLANES, D_CONV), F32)
        for tap in range(CONV_WIDTH):
            off = r0 + tap + CONV_HALO - (CONV_WIDTH - 1)
            sh, al = off % SUBLANES, (off // SUBLANES) * SUBLANES
            if sh == 0:
                blk = ubuf[al:al + CONV_ROWS, :]
            else:
                blk = shifted[sh - 1, al:al + CONV_ROWS, :]
            acc = acc + blk.reshape(CONV_ROWS // SUBLANES, SUBLANES, D_CONV) * cw_ref[tap]
        y = acc.reshape(CONV_ROWS, D_CONV) + cb_ref[...]
        mu = jnp.mean(y, axis=-1, keepdims=True)
        yc = y - mu
        yn = yc * lax.rsqrt(jnp.mean(yc * yc, axis=-1, keepdims=True) + EPS) * lng_ref[...] + lnb_ref[...]
        sw = yn * jax.nn.sigmoid(yn)
        u_ref[0, r0:r0 + CONV_ROWS, :] = _rms(sw, cog_ref[...]).astype(BF16)

    ubuf[0:CONV_HALO, :] = ubuf[TOK_TILE:TOK_TILE + CONV_HALO, :]


def _mix_in(x, g, wqkv, wf, bf, tri, wag, cw, cb, lng, lnb, cog):
    batch, seq, _ = x.shape
    tok = lambda w: pl.BlockSpec((1, TOK_TILE, w), lambda b, s: (b, s, 0))
    n_shift = TOK_TILE + CONV_HALO - SUBLANES
    outs = (jax.ShapeDtypeStruct((batch, seq, D_ATTN), BF16),) * 3 + (
        jax.ShapeDtypeStruct((batch, N_ATTN_HEADS, seq), F32),
        jax.ShapeDtypeStruct((batch, seq, D_CONV), BF16))
    return pl.pallas_call(
        _mix_in_kernel,
        out_shape=outs,
        grid=(batch, seq // TOK_TILE),
        in_specs=[tok(D_MODEL), _const_spec((1, D_MODEL)), _const_spec((D_MODEL, 3 * D_ATTN)),
                  _const_spec((D_MODEL, LANES)), _const_spec((N_ATTN_HEADS, 1)),
                  _const_spec((TOK_TILE, TOK_TILE)),
                  _const_spec((D_MODEL, 2 * D_CONV)), _const_spec((CONV_WIDTH, SUBLANES, D_CONV)),
                  _const_spec((1, D_CONV)), _const_spec((1, D_CONV)), _const_spec((1, D_CONV)),
                  _const_spec((1, D_CONV))],
        out_specs=(tok(D_ATTN), tok(D_ATTN), tok(D_ATTN),
                   pl.BlockSpec((1, N_ATTN_HEADS, TOK_TILE), lambda b, s: (b, 0, s)), tok(D_CONV)),
        scratch_shapes=[pltpu.VMEM((TOK_TILE, D_MODEL), BF16),
                        pltpu.VMEM((TOK_TILE + CONV_HALO, D_CONV), F32),
                        pltpu.VMEM((SUBLANES - 1, n_shift, D_CONV), F32),
                        pltpu.VMEM((N_ATTN_HEADS, 1), F32)],
        compiler_params=pltpu.CompilerParams(
            dimension_semantics=("arbitrary", "arbitrary"), vmem_limit_bytes=VMEM_LIMIT),
        name="mix_in",
    )(x, g, wqkv, wf, bf, tri, wag, cw, cb, lng, lnb, cog)


def _fox_kernel(q_ref, k_ref, v_ref, ccol_ref, crow_ref, g_ref, o_ref,
                qm_scr, ct_scr, m_scr, l_scr, acc_scr, sa_scr, sb_scr):
    i = pl.program_id(1)
    n_chunk = ATT_TILE // LANES
    lane = lax.broadcasted_iota(jnp.int32, (ATT_TILE, LANES), 1)
    first = lane < HEAD_DIM
    row = lax.broadcasted_iota(jnp.int32, (ATT_TILE, LANES), 0)

    c_t = ccol_ref[0]
    for p in range(N_PAIRS):
        qp = q_ref[0, :, p * LANES:(p + 1) * LANES]
        zero = jnp.zeros_like(qp)
        qm_scr[2 * p] = jnp.where(first, qp, zero)
        qm_scr[2 * p + 1] = jnp.where(first, zero, qp)
    for h in range(N_ATTN_HEADS):
        ct_scr[h] = jnp.broadcast_to(c_t[:, h:h + 1], (ATT_TILE, LANES))
    m_scr[...] = jnp.full(m_scr.shape, NEG_INF, F32)
    l_scr[...] = jnp.zeros(l_scr.shape, F32)
    acc_scr[...] = jnp.zeros(acc_scr.shape, F32)

    def scores(j, s_ref):
        k0 = pl.multiple_of(j * ATT_TILE, ATT_TILE)
        c_s = crow_ref[0, j]
        for p in range(N_PAIRS):
            kb = k_ref[0, pl.ds(k0, ATT_TILE), p * LANES:(p + 1) * LANES]
            for h in (2 * p, 2 * p + 1):
                s = lax.dot_general(qm_scr[h], kb, (((1,), (1,)), ((), ())), preferred_element_type=F32)
                ct = ct_scr[h]
                for c in range(n_chunk):
                    cols = slice(c * LANES, (c + 1) * LANES)
                    s_ref[h, :, cols] = s[:, cols] + ct - c_s[h:h + 1, cols]

    def update(j, s_ref, diagonal):
        k0 = pl.multiple_of(j * ATT_TILE, ATT_TILE)

        def logits(h, c):
            sc = s_ref[h, :, c * LANES:(c + 1) * LANES]
            return jnp.where(row >= lane + c * LANES, sc, NEG_INF) if diagonal else sc

        for p in range(N_PAIRS):
            vb = v_ref[0, pl.ds(k0, ATT_TILE), p * LANES:(p + 1) * LANES]
            pv, alphas = [], []
            for h in (2 * p, 2 * p + 1):
                m_old = m_scr[h]
                m_blk = functools.reduce(jnp.maximum, [logits(h, c) for c in range(n_chunk)])
                m_new = jnp.maximum(m_old, jnp.max(m_blk, axis=-1, keepdims=True))
                alpha = jnp.exp2(m_old - m_new)
                ps = [jnp.exp2(logits(h, c) - m_new) for c in range(n_chunk)]
                l_scr[h] = alpha * l_scr[h] + functools.reduce(jnp.add, ps)
                m_scr[h] = m_new
                p_bf = jnp.concatenate(ps, axis=1).astype(BF16)
                pv.append(jnp.dot(p_bf, vb, preferred_element_type=F32))
                alphas.append(alpha)
            acc_scr[p] = (jnp.where(first, alphas[0], alphas[1]) * acc_scr[p]
                          + jnp.where(first, pv[0], pv[1]))

    def two_blocks(t, _):
        j = 2 * t
        scores(j + 1, sb_scr)
        update(j, sa_scr, False)
        scores(j + 2, sa_scr)
        update(j + 1, sb_scr, False)
        return 0

    scores(0, sa_scr)
    lax.fori_loop(0, i // 2, two_blocks, 0)

    @pl.when(i % 2 == 0)
    def _():
        update(i, sa_scr, True)

    @pl.when(i % 2 == 1)
    def _():
        scores(i, sb_scr)
        update(i - 1, sa_scr, False)
        update(i, sb_scr, True)

    outs = []
    for p in range(N_PAIRS):
        l0 = jnp.sum(l_scr[2 * p], axis=-1, keepdims=True)
        l1 = jnp.sum(l_scr[2 * p + 1], axis=-1, keepdims=True)
        outs.append(acc_scr[p] / jnp.where(first, l0, l1))
    o_ref[0] = _rms(jnp.concatenate(outs, axis=1), g_ref[...]).astype(BF16)


def _fox_attention(q, k, v, ccol, crow, g):
    batch, seq, _ = q.shape
    n_blk = seq // ATT_TILE
    stat = pltpu.VMEM((N_ATTN_HEADS, ATT_TILE, LANES), F32)
    logit_buf = pltpu.VMEM((N_ATTN_HEADS, ATT_TILE, ATT_TILE), F32)
    return pl.pallas_call(
        _fox_kernel,
        out_shape=jax.ShapeDtypeStruct((batch, seq, D_ATTN), BF16),
        grid=(batch, n_blk),
        in_specs=[pl.BlockSpec((1, ATT_TILE, D_ATTN), lambda b, i: (b, i, 0)),
                  pl.BlockSpec((1, seq, D_ATTN), lambda b, i: (b, 0, 0)),
                  pl.BlockSpec((1, seq, D_ATTN), lambda b, i: (b, 0, 0)),
                  pl.BlockSpec((1, ATT_TILE, N_ATTN_HEADS), lambda b, i: (b, i, 0)),
                  pl.BlockSpec((1, n_blk, N_ATTN_HEADS, ATT_TILE), lambda b, i: (b, 0, 0, 0)),
                  _const_spec((1, D_ATTN))],
        out_specs=pl.BlockSpec((1, ATT_TILE, D_ATTN), lambda b, i: (b, i, 0)),
        scratch_shapes=[pltpu.VMEM((N_ATTN_HEADS, ATT_TILE, LANES), BF16), stat, stat, stat,
                        pltpu.VMEM((N_PAIRS, ATT_TILE, LANES), F32), logit_buf, logit_buf],
        compiler_params=pltpu.CompilerParams(
            dimension_semantics=("arbitrary", "arbitrary"), vmem_limit_bytes=VMEM_LIMIT),
        name="fox_attn",
    )(q, k, v, ccol, crow, g)


def _mix_out_kernel(x_ref, a_ref, u_ref, wout_ref, xg_ref, wq_ref, mk_ref, mv_ref, wo_ref,
                    o_ref, o_scr):
    y = jnp.dot(a_ref[0], wout_ref[:D_ATTN, :], preferred_element_type=F32)
    y = y + jnp.dot(u_ref[0], wout_ref[D_ATTN:, :], preferred_element_type=F32)
    x = x_ref[0] + y

    hq = _rms(x, xg_ref[...]).astype(BF16)
    q = jnp.dot(hq, wq_ref[...], preferred_element_type=F32) * (1.0 / math.sqrt(XATTN_HEAD_DIM))
    q = q.astype(BF16)
    for h in range(N_XATTN_HEADS):
        cols = slice(h * XATTN_HEAD_DIM, (h + 1) * XATTN_HEAD_DIM)
        s = lax.dot_general(q[:, cols], mk_ref[0, :, cols], (((1,), (1,)), ((), ())),
                            preferred_element_type=F32)
        p = jnp.exp(s - jnp.max(s, axis=-1, keepdims=True))
        p = p / jnp.sum(p, axis=-1, keepdims=True)
        o_scr[:, cols] = jnp.dot(p.astype(BF16), mv_ref[0, :, cols],
                                 preferred_element_type=F32).astype(BF16)
    o_ref[0] = x + jnp.dot(o_scr[...], wo_ref[...], preferred_element_type=F32)


def _mix_out(x, attn, u, wout, xg, wq, mk, mv, wo):
    batch, seq, _ = x.shape
    tok = lambda w: pl.BlockSpec((1, TOK_TILE, w), lambda b, s: (b, s, 0))
    mem_spec = pl.BlockSpec((1, N_MEM, D_MODEL), lambda b, s: (b, 0, 0))
    return pl.pallas_call(
        _mix_out_kernel,
        out_shape=jax.ShapeDtypeStruct((batch, seq, D_MODEL), F32),
        grid=(batch, seq // TOK_TILE),
        in_specs=[tok(D_MODEL), tok(D_ATTN), tok(D_CONV),
                  _const_spec((D_MODEL, D_MODEL)), _const_spec((1, D_MODEL)),
                  _const_spec((D_MODEL, D_MODEL)), mem_spec, mem_spec,
                  _const_spec((D_MODEL, D_MODEL))],
        out_specs=tok(D_MODEL),
        scratch_shapes=[pltpu.VMEM((TOK_TILE, D_MODEL), BF16)],
        compiler_params=pltpu.CompilerParams(
            dimension_semantics=("arbitrary", "arbitrary"), vmem_limit_bytes=VMEM_LIMIT),
        name="mix_out",
    )(x, attn, u, wout, xg, wq, mk, mv, wo)


def kernel(x, mem, ffn1_norm_g, ffn1_w_gate, ffn1_w_up, ffn1_w_down, mix_norm_g, w_in, b_f, conv_w, conv_b, conv_ln_g, conv_ln_b, attn_out_g, conv_out_g, w_out, xattn_norm_g, mem_norm_g, xattn_w_q, xattn_w_kv, xattn_w_o, ffn2_norm_g, ffn2_w_gate, ffn2_w_up, ffn2_w_down, final_norm_g):
    batch, seq, _ = x.shape
    depth = w_in.shape[0]
    n_blk = seq // ATT_TILE
    row = lambda v: v.reshape(1, -1)

    mem_k, mem_v = _memkv(mem, mem_norm_g.reshape(depth, 1, D_MODEL), xattn_w_kv.astype(BF16))
    fg = row(final_norm_g)
    tri = jnp.triu(jnp.ones((TOK_TILE, TOK_TILE), BF16))

    for l in range(depth):
        x2d = _ffn(x.reshape(batch * seq, D_MODEL), row(ffn1_norm_g[l]), ffn1_w_gate[l].astype(BF16),
                   ffn1_w_up[l].astype(BF16), ffn1_w_down[l].astype(BF16), fg, final=False)
        x = x2d.reshape(batch, seq, D_MODEL)

        w = w_in[l]
        wf = jnp.pad(w[:, 3 * D_ATTN:3 * D_ATTN + N_ATTN_HEADS], ((0, 0), (0, LANES - N_ATTN_HEADS)))
        cw = jnp.broadcast_to(conv_w[l][:, None, :], (CONV_WIDTH, SUBLANES, D_CONV))
        q, k, v, c, u = _mix_in(
            x, row(mix_norm_g[l]), w[:, :3 * D_ATTN].astype(BF16), wf.astype(BF16),
            b_f[l].reshape(N_ATTN_HEADS, 1), tri,
            w[:, 3 * D_ATTN + N_ATTN_HEADS:].astype(BF16), cw, row(conv_b[l]), row(conv_ln_g[l]),
            row(conv_ln_b[l]), row(conv_out_g[l]))

        ccol = c.transpose(0, 2, 1)
        crow = c.reshape(batch, N_ATTN_HEADS, n_blk, ATT_TILE).transpose(0, 2, 1, 3)
        attn = _fox_attention(q, k, v, ccol, crow, row(attn_out_g[l]))

        x = _mix_out(x, attn, u, w_out[l].astype(BF16), row(xattn_norm_g[l]),
                     xattn_w_q[l].astype(BF16), mem_k[l], mem_v[l], xattn_w_o[l].astype(BF16))

        last = l == depth - 1
        x2d = _ffn(x.reshape(batch * seq, D_MODEL), row(ffn2_norm_g[l]), ffn2_w_gate[l].astype(BF16),
                   ffn2_w_up[l].astype(BF16), ffn2_w_down[l].astype(BF16), fg, final=last)
        x = x2d.reshape(batch, seq, D_MODEL)
    return x
```

```python
import functools
import math

import jax
import jax.numpy as jnp
from jax import lax
from jax.experimental import pallas as pl
from jax.experimental.pallas import tpu as pltpu

F32 = jnp.float32
BF16 = jnp.bfloat16

D_MODEL = 1024
N_MEM = 256
D_ATTN = 512
D_CONV = 512
HEAD_DIM = 64
N_ATTN_HEADS = 8
CONV_WIDTH = 31
N_XATTN_HEADS = 4
XATTN_HEAD_DIM = 256
D_FF = 2816
EPS = 1e-6
NEG_INF = -1e30
LOG2E = math.log2(math.e)

LANES = 128
SUBLANES = 8
MXU_DIM = 256
VMEM_LIMIT = 56 * 1024 * 1024

TOK_TILE = 512
ATT_TILE = 256
KEY_TILE = 2 * ATT_TILE
FF_CHUNK = MXU_DIM
CONV_HALO = 32
CONV_ROWS = 64
N_PAIRS = D_ATTN // LANES


def _rms(x, g):
    return x * lax.rsqrt(jnp.mean(x * x, axis=-1, keepdims=True) + EPS) * g


def _const_spec(shape):
    nd = len(shape)
    return pl.BlockSpec(shape, lambda *_: (0,) * nd, pipeline_mode=pl.Buffered(1))


def _ffn_kernel(x_ref, g_ref, wg_ref, wu_ref, wd_ref, fg_ref, o_ref, h_scr, act_scr, *, final):
    x = x_ref[...]
    h_scr[...] = _rms(x, g_ref[...]).astype(BF16)
    for c in range(D_FF // FF_CHUNK):
        cols = slice(c * FF_CHUNK, (c + 1) * FF_CHUNK)
        h = h_scr[...]
        gate = jnp.dot(h, wg_ref[:, cols], preferred_element_type=F32)
        up = jnp.dot(h, wu_ref[:, cols], preferred_element_type=F32)
        act_scr[:, cols] = (gate * jax.nn.sigmoid(gate) * up).astype(BF16)
    y = jnp.dot(act_scr[...], wd_ref[...], preferred_element_type=F32)
    out = x + 0.5 * y
    if final:
        out = _rms(out, fg_ref[...])
    o_ref[...] = out


def _ffn(x2d, g, wg, wu, wd, fg, *, final):
    t = x2d.shape[0]
    tile = pl.BlockSpec((TOK_TILE, D_MODEL), lambda i: (i, 0))
    return pl.pallas_call(
        functools.partial(_ffn_kernel, final=final),
        out_shape=jax.ShapeDtypeStruct((t, D_MODEL), F32),
        grid=(t // TOK_TILE,),
        in_specs=[tile, _const_spec((1, D_MODEL)), _const_spec((D_MODEL, D_FF)),
                  _const_spec((D_MODEL, D_FF)), _const_spec((D_FF, D_MODEL)),
                  _const_spec((1, D_MODEL))],
        out_specs=tile,
        scratch_shapes=[pltpu.VMEM((TOK_TILE, D_MODEL), BF16), pltpu.VMEM((TOK_TILE, D_FF), BF16)],
        compiler_params=pltpu.CompilerParams(
            dimension_semantics=("arbitrary",), vmem_limit_bytes=VMEM_LIMIT),
        name="ffn_final" if final else "ffn",
    )(x2d, g, wg, wu, wd, fg)


def _memkv_kernel(mem_ref, g_ref, w_ref, k_ref, v_ref):
    mn = _rms(mem_ref[0], g_ref[0]).astype(BF16)
    kv = jnp.dot(mn, w_ref[0], preferred_element_type=F32)
    k_ref[0, 0] = kv[:, :D_MODEL].astype(BF16)
    v_ref[0, 0] = kv[:, D_MODEL:].astype(BF16)


def _memkv(mem, g, w_kv):
    depth, batch = g.shape[0], mem.shape[0]
    out = jax.ShapeDtypeStruct((depth, batch, N_MEM, D_MODEL), BF16)
    out_spec = pl.BlockSpec((1, 1, N_MEM, D_MODEL), lambda l, b: (l, b, 0, 0))
    return pl.pallas_call(
        _memkv_kernel,
        out_shape=(out, out),
        grid=(depth, batch),
        in_specs=[pl.BlockSpec((1, N_MEM, D_MODEL), lambda l, b: (b, 0, 0)),
                  pl.BlockSpec((1, 1, D_MODEL), lambda l, b: (l, 0, 0)),
                  pl.BlockSpec((1, D_MODEL, 2 * D_MODEL), lambda l, b: (l, 0, 0))],
        out_specs=(out_spec, out_spec),
        compiler_params=pltpu.CompilerParams(
            dimension_semantics=("arbitrary", "arbitrary"), vmem_limit_bytes=VMEM_LIMIT),
        name="mem_kv",
    )(mem, g, w_kv)


def _log_sigmoid(z):
    return jnp.minimum(z, 0.0) - jnp.log1p(jnp.exp(-jnp.abs(z)))


def _split3(a):
    hi = a.astype(BF16).astype(F32)
    r = a - hi
    mid = r.astype(BF16).astype(F32)
    lo = (r - mid).astype(BF16).astype(F32)
    return hi, mid, lo


def _mix_in_kernel(x_ref, g_ref, wqkv_ref, wf_ref, bf_ref, tri_ref, wag_ref, cw_ref, cb_ref, lng_ref,
                   lnb_ref, cog_ref, q_ref, k_ref, v_ref, c_ref, u_ref,
                   h_scr, ubuf, shifted, carry):
    s_idx = pl.program_id(1)

    @pl.when(s_idx == 0)
    def _():
        carry[...] = jnp.zeros_like(carry)
        ubuf[0:CONV_HALO, :] = jnp.zeros((CONV_HALO, D_CONV), F32)

    h_scr[...] = _rms(x_ref[0], g_ref[...]).astype(BF16)

    ag = jnp.dot(h_scr[...], wag_ref[...], preferred_element_type=F32)
    ubuf[CONV_HALO:, :] = ag[:, :D_CONV] * jax.nn.sigmoid(ag[:, D_CONV:])
    n_shift = TOK_TILE + CONV_HALO - SUBLANES
    for r in range(1, SUBLANES):
        shifted[r - 1] = ubuf[r:r + n_shift, :]

    qkv = jnp.dot(h_scr[...], wqkv_ref[...], preferred_element_type=F32)
    q_ref[0] = (qkv[:, :D_ATTN] * (LOG2E / math.sqrt(HEAD_DIM))).astype(BF16)
    k_ref[0] = qkv[:, D_ATTN:2 * D_ATTN].astype(BF16)
    v_ref[0] = qkv[:, 2 * D_ATTN:].astype(BF16)

    f_t = jnp.transpose(jnp.dot(h_scr[...], wf_ref[...], preferred_element_type=F32))
    logf = _log_sigmoid(f_t[:N_ATTN_HEADS, :] + bf_ref[...]) * LOG2E
    parts = jnp.concatenate(_split3(logf) + (jnp.zeros_like(logf),), axis=0).astype(BF16)
    sums = jnp.dot(parts, tri_ref[...], preferred_element_type=F32)
    h8 = N_ATTN_HEADS
    c_tile = carry[...] + sums[:h8] + sums[h8:2 * h8] + sums[2 * h8:3 * h8]
    c_ref[0] = c_tile
    carry[...] = c_tile[:, TOK_TILE - 1:TOK_TILE]

    for rb in range(TOK_TILE // CONV_ROWS):
        r0 = rb * CONV_ROWS
        acc = jnp.zeros((CONV_ROWS // SUBLANES, SUBLANES, D_CONV), F32)
        for tap in range(CONV_WIDTH):
            off = r0 + tap + CONV_HALO - (CONV_WIDTH - 1)
            sh, al = off % SUBLANES, (off // SUBLANES) * SUBLANES
            if sh == 0:
                blk = ubuf[al:al + CONV_ROWS, :]
            else:
                blk = shifted[sh - 1, al:al + CONV_ROWS, :]
            acc = acc + blk.reshape(CONV_ROWS // SUBLANES, SUBLANES, D_CONV) * cw_ref[tap]
        y = acc.reshape(CONV_ROWS, D_CONV) + cb_ref[...]
        mu = jnp.mean(y, axis=-1, keepdims=True)
        yc = y - mu
        yn = yc * lax.rsqrt(jnp.mean(yc * yc, axis=-1, keepdims=True) + EPS) * lng_ref[...] + lnb_ref[...]
        sw = yn * jax.nn.sigmoid(yn)
        u_ref[0, r0:r0 + CONV_ROWS, :] = _rms(sw, cog_ref[...]).astype(BF16)

    ubuf[0:CONV_HALO, :] = ubuf[TOK_TILE:TOK_TILE + CONV_HALO, :]


def _mix_in(x, g, wqkv, wf, bf, tri, wag, cw, cb, lng, lnb, cog):
    batch, seq, _ = x.shape
    tok = lambda w: pl.BlockSpec((1, TOK_TILE, w), lambda b, s: (b, s, 0))
    n_shift = TOK_TILE + CONV_HALO - SUBLANES
    outs = (jax.ShapeDtypeStruct((batch, seq, D_ATTN), BF16),) * 3 + (
        jax.ShapeDtypeStruct((batch, N_ATTN_HEADS, seq), F32),
        jax.ShapeDtypeStruct((batch, seq, D_CONV), BF16))
    return pl.pallas_call(
        _mix_in_kernel,
        out_shape=outs,
        grid=(batch, seq // TOK_TILE),
        in_specs=[tok(D_MODEL), _const_spec((1, D_MODEL)), _const_spec((D_MODEL, 3 * D_ATTN)),
                  _const_spec((D_MODEL, LANES)), _const_spec((N_ATTN_HEADS, 1)),
                  _const_spec((TOK_TILE, TOK_TILE)),
                  _const_spec((D_MODEL, 2 * D_CONV)), _const_spec((CONV_WIDTH, SUBLANES, D_CONV)),
                  _const_spec((1, D_CONV)), _const_spec((1, D_CONV)), _const_spec((1, D_CONV)),
                  _const_spec((1, D_CONV))],
        out_specs=(tok(D_ATTN), tok(D_ATTN), tok(D_ATTN),
                   pl.BlockSpec((1, N_ATTN_HEADS, TOK_TILE), lambda b, s: (b, 0, s)), tok(D_CONV)),
        scratch_shapes=[pltpu.VMEM((TOK_TILE, D_MODEL), BF16),
                        pltpu.VMEM((TOK_TILE + CONV_HALO, D_CONV), F32),
                        pltpu.VMEM((SUBLANES - 1, n_shift, D_CONV), F32),
                        pltpu.VMEM((N_ATTN_HEADS, 1), F32)],
        compiler_params=pltpu.CompilerParams(
            dimension_semantics=("arbitrary", "arbitrary"), vmem_limit_bytes=VMEM_LIMIT),
        name="mix_in",
    )(x, g, wqkv, wf, bf, tri, wag, cw, cb, lng, lnb, cog)


def _fox_kernel(q_ref, k_ref, v_ref, ccol_ref, crow_ref, g_ref, o_ref,
                qm_scr, ct_scr, m_scr, l_scr, acc_scr, s_scr):
    i = pl.program_id(1)
    n_full = i // 2
    lane = lax.broadcasted_iota(jnp.int32, (ATT_TILE, LANES), 1)
    first = lane < HEAD_DIM
    row = lax.broadcasted_iota(jnp.int32, (ATT_TILE, LANES), 0)

    c_t = ccol_ref[0]
    for p in range(N_PAIRS):
        qp = q_ref[0, :, p * LANES:(p + 1) * LANES]
        zero = jnp.zeros_like(qp)
        qm_scr[2 * p] = jnp.where(first, qp, zero)
        qm_scr[2 * p + 1] = jnp.where(first, zero, qp)
    for h in range(N_ATTN_HEADS):
        ct_scr[h] = jnp.broadcast_to(c_t[:, h:h + 1], (ATT_TILE, LANES))
    m_scr[...] = jnp.full(m_scr.shape, NEG_INF, F32)
    l_scr[...] = jnp.zeros(l_scr.shape, F32)
    acc_scr[...] = jnp.zeros(acc_scr.shape, F32)

    def scores(t, h, n_chunk):
        k0 = pl.multiple_of(t * KEY_TILE, KEY_TILE)
        pair = slice((h // 2) * LANES, (h // 2 + 1) * LANES)
        c_s = crow_ref[0, t]
        s = lax.dot_general(qm_scr[h], k_ref[0, pl.ds(k0, n_chunk * LANES), pair],
                            (((1,), (1,)), ((), ())), preferred_element_type=F32)
        ct = ct_scr[h]
        for c in range(n_chunk):
            cols = slice(c * LANES, (c + 1) * LANES)
            s_scr[h, :, cols] = s[:, cols] + ct - c_s[h:h + 1, cols]

    def update(t, h, n_chunk, diag_from):
        k0 = pl.multiple_of(t * KEY_TILE, KEY_TILE)
        pair = slice((h // 2) * LANES, (h // 2 + 1) * LANES)

        def logits(c):
            sc = s_scr[h, :, c * LANES:(c + 1) * LANES]
            if c >= diag_from:
                sc = jnp.where(row >= lane + (c - diag_from) * LANES, sc, NEG_INF)
            return sc

        m_old = m_scr[h]
        m_blk = functools.reduce(jnp.maximum, [logits(c) for c in range(n_chunk)])
        m_new = jnp.maximum(m_old, jnp.max(m_blk, axis=-1, keepdims=True))
        alpha = jnp.exp2(m_old - m_new)
        ps = [jnp.exp2(logits(c) - m_new) for c in range(n_chunk)]
        l_scr[h] = alpha * l_scr[h] + functools.reduce(jnp.add, ps)
        m_scr[h] = m_new
        acc_scr[h] = alpha * acc_scr[h]
        p_bf = jnp.concatenate(ps, axis=1).astype(BF16)
        acc_scr[h] += jnp.dot(p_bf, v_ref[0, pl.ds(k0, n_chunk * LANES), pair],
                              preferred_element_type=F32)

    all_chunks = KEY_TILE // LANES
    half_chunks = ATT_TILE // LANES
    last = N_ATTN_HEADS - 1

    def full_window(t, _):
        for h in range(N_ATTN_HEADS):
            if h < last:
                scores(t, h + 1, all_chunks)
            else:
                scores(t + 1, 0, all_chunks)
            update(t, h, all_chunks, all_chunks)
        return 0

    scores(0, 0, all_chunks)
    lax.fori_loop(0, n_full, full_window, 0)

    @pl.when(i % 2 == 0)
    def _():
        for h in range(N_ATTN_HEADS):
            if h < last:
                scores(n_full, h + 1, half_chunks)
            update(n_full, h, half_chunks, 0)

    @pl.when(i % 2 == 1)
    def _():
        for h in range(N_ATTN_HEADS):
            if h < last:
                scores(n_full, h + 1, all_chunks)
            update(n_full, h, all_chunks, half_chunks)

    outs = []
    for p in range(N_PAIRS):
        l0 = jnp.sum(l_scr[2 * p], axis=-1, keepdims=True)
        l1 = jnp.sum(l_scr[2 * p + 1], axis=-1, keepdims=True)
        outs.append(jnp.where(first, acc_scr[2 * p] / l0, acc_scr[2 * p + 1] / l1))
    o_ref[0] = _rms(jnp.concatenate(outs, axis=1), g_ref[...]).astype(BF16)


def _fox_attention(q, k, v, ccol, crow, g):
    batch, seq, _ = q.shape
    n_win = seq // KEY_TILE
    stat = pltpu.VMEM((N_ATTN_HEADS, ATT_TILE, LANES), F32)
    return pl.pallas_call(
        _fox_kernel,
        out_shape=jax.ShapeDtypeStruct((batch, seq, D_ATTN), BF16),
        grid=(batch, seq // ATT_TILE),
        in_specs=[pl.BlockSpec((1, ATT_TILE, D_ATTN), lambda b, i: (b, i, 0)),
                  pl.BlockSpec((1, seq, D_ATTN), lambda b, i: (b, 0, 0)),
                  pl.BlockSpec((1, seq, D_ATTN), lambda b, i: (b, 0, 0)),
                  pl.BlockSpec((1, ATT_TILE, N_ATTN_HEADS), lambda b, i: (b, i, 0)),
                  pl.BlockSpec((1, n_win, N_ATTN_HEADS, KEY_TILE), lambda b, i: (b, 0, 0, 0)),
                  _const_spec((1, D_ATTN))],
        out_specs=pl.BlockSpec((1, ATT_TILE, D_ATTN), lambda b, i: (b, i, 0)),
        scratch_shapes=[pltpu.VMEM((N_ATTN_HEADS, ATT_TILE, LANES), BF16), stat, stat, stat, stat,
                        pltpu.VMEM((N_ATTN_HEADS, ATT_TILE, KEY_TILE), F32)],
        compiler_params=pltpu.CompilerParams(
            dimension_semantics=("arbitrary", "arbitrary"), vmem_limit_bytes=VMEM_LIMIT),
        name="fox_attn",
    )(q, k, v, ccol, crow, g)


def _mix_out_kernel(x_ref, a_ref, u_ref, wout_ref, xg_ref, wq_ref, mk_ref, mv_ref, wo_ref,
                    o_ref, o_scr):
    y = jnp.dot(a_ref[0], wout_ref[:D_ATTN, :], preferred_element_type=F32)
    y = y + jnp.dot(u_ref[0], wout_ref[D_ATTN:, :], preferred_element_type=F32)
    x = x_ref[0] + y

    hq = _rms(x, xg_ref[...]).astype(BF16)
    q = jnp.dot(hq, wq_ref[...], preferred_element_type=F32) * (1.0 / math.sqrt(XATTN_HEAD_DIM))
    q = q.astype(BF16)
    for h in range(N_XATTN_HEADS):
        cols = slice(h * XATTN_HEAD_DIM, (h + 1) * XATTN_HEAD_DIM)
        s = lax.dot_general(q[:, cols], mk_ref[0, :, cols], (((1,), (1,)), ((), ())),
                            preferred_element_type=F32)
        p = jnp.exp(s - jnp.max(s, axis=-1, keepdims=True))
        p = p / jnp.sum(p, axis=-1, keepdims=True)
        o_scr[:, cols] = jnp.dot(p.astype(BF16), mv_ref[0, :, cols],
                                 preferred_element_type=F32).astype(BF16)
    o_ref[0] = x + jnp.dot(o_scr[...], wo_ref[...], preferred_element_type=F32)


def _mix_out(x, attn, u, wout, xg, wq, mk, mv, wo):
    batch, seq, _ = x.shape
    tok = lambda w: pl.BlockSpec((1, TOK_TILE, w), lambda b, s: (b, s, 0))
    mem_spec = pl.BlockSpec((1, N_MEM, D_MODEL), lambda b, s: (b, 0, 0))
    return pl.pallas_call(
        _mix_out_kernel,
        out_shape=jax.ShapeDtypeStruct((batch, seq, D_MODEL), F32),
        grid=(batch, seq // TOK_TILE),
        in_specs=[tok(D_MODEL), tok(D_ATTN), tok(D_CONV),
                  _const_spec((D_MODEL, D_MODEL)), _const_spec((1, D_MODEL)),
                  _const_spec((D_MODEL, D_MODEL)), mem_spec, mem_spec,
                  _const_spec((D_MODEL, D_MODEL))],
        out_specs=tok(D_MODEL),
        scratch_shapes=[pltpu.VMEM((TOK_TILE, D_MODEL), BF16)],
        compiler_params=pltpu.CompilerParams(
            dimension_semantics=("arbitrary", "arbitrary"), vmem_limit_bytes=VMEM_LIMIT),
        name="mix_out",
    )(x, attn, u, wout, xg, wq, mk, mv, wo)


def kernel(x, mem, ffn1_norm_g, ffn1_w_gate, ffn1_w_up, ffn1_w_down, mix_norm_g, w_in, b_f, conv_w, conv_b, conv_ln_g, conv_ln_b, attn_out_g, conv_out_g, w_out, xattn_norm_g, mem_norm_g, xattn_w_q, xattn_w_kv, xattn_w_o, ffn2_norm_g, ffn2_w_gate, ffn2_w_up, ffn2_w_down, final_norm_g):
    batch, seq, _ = x.shape
    depth = w_in.shape[0]
    n_win = seq // KEY_TILE
    row = lambda v: v.reshape(1, -1)

    mem_k, mem_v = _memkv(mem, mem_norm_g.reshape(depth, 1, D_MODEL), xattn_w_kv.astype(BF16))
    fg = row(final_norm_g)
    tri = jnp.triu(jnp.ones((TOK_TILE, TOK_TILE), BF16))

    for l in range(depth):
        x2d = _ffn(x.reshape(batch * seq, D_MODEL), row(ffn1_norm_g[l]), ffn1_w_gate[l].astype(BF16),
                   ffn1_w_up[l].astype(BF16), ffn1_w_down[l].astype(BF16), fg, final=False)
        x = x2d.reshape(batch, seq, D_MODEL)

        w = w_in[l]
        wf = jnp.pad(w[:, 3 * D_ATTN:3 * D_ATTN + N_ATTN_HEADS], ((0, 0), (0, LANES - N_ATTN_HEADS)))
        cw = jnp.broadcast_to(conv_w[l][:, None, :], (CONV_WIDTH, SUBLANES, D_CONV))
        q, k, v, c, u = _mix_in(
            x, row(mix_norm_g[l]), w[:, :3 * D_ATTN].astype(BF16), wf.astype(BF16),
            b_f[l].reshape(N_ATTN_HEADS, 1), tri,
            w[:, 3 * D_ATTN + N_ATTN_HEADS:].astype(BF16), cw, row(conv_b[l]), row(conv_ln_g[l]),
            row(conv_ln_b[l]), row(conv_out_g[l]))

        ccol = c.transpose(0, 2, 1)
        crow = c.reshape(batch, N_ATTN_HEADS, n_win, KEY_TILE).transpose(0, 2, 1, 3)
        attn = _fox_attention(q, k, v, ccol, crow, row(attn_out_g[l]))

        x = _mix_out(x, attn, u, w_out[l].astype(BF16), row(xattn_norm_g[l]),
                     xattn_w_q[l].astype(BF16), mem_k[l], mem_v[l], xattn_w_o[l].astype(BF16))

        last = l == depth - 1
        x2d = _ffn(x.reshape(batch * seq, D_MODEL), row(ffn2_norm_g[l]), ffn2_w_gate[l].astype(BF16),
                   ffn2_w_up[l].astype(BF16), ffn2_w_down[l].astype(BF16), fg, final=last)
        x = x2d.reshape(batch, seq, D_MODEL)
    return x
```

```python
import functools
import math

import jax
import jax.numpy as jnp
from jax import lax
from jax.experimental import pallas as pl
from jax.experimental.pallas import tpu as pltpu

F32 = jnp.float32
BF16 = jnp.bfloat16

D_MODEL = 1024
N_MEM = 256
D_ATTN = 512
D_CONV = 512
HEAD_DIM = 64
N_ATTN_HEADS = 8
CONV_WIDTH = 31
N_XATTN_HEADS = 4
XATTN_HEAD_DIM = 256
D_FF = 2816
D_IN = 3 * D_ATTN + N_ATTN_HEADS + 2 * D_CONV
EPS = 1e-6
NEG_INF = -1e30
LOG2E = math.log2(math.e)

LANES = 128
SUBLANES = 8
MXU_DIM = 256
VMEM_LIMIT = 56 * 1024 * 1024

TOK_TILE = 512
ATT_TILE = 256
KEY_TILE = 2 * ATT_TILE
FF_CHUNK = MXU_DIM
CONV_HALO = 32
CONV_ROWS = 64
N_PAIRS = D_ATTN // LANES


def _rms(x, g):
    return x * lax.rsqrt(jnp.mean(x * x, axis=-1, keepdims=True) + EPS) * g


def _const_spec(shape):
    nd = len(shape)
    return pl.BlockSpec(shape, lambda *_: (0,) * nd, pipeline_mode=pl.Buffered(1))


def _layer_spec(shape, layer):
    nd = len(shape)
    return pl.BlockSpec((1,) + shape, lambda *_: (layer,) + (0,) * nd, pipeline_mode=pl.Buffered(1))


def _ffn_kernel(x_ref, g_ref, wg_ref, wu_ref, wd_ref, fg_ref, o_ref, h_scr, act_scr, *, final):
    x = x_ref[...]
    h_scr[...] = _rms(x, g_ref[...]).astype(BF16)
    for c in range(D_FF // FF_CHUNK):
        cols = slice(c * FF_CHUNK, (c + 1) * FF_CHUNK)
        h = h_scr[...]
        gate = jnp.dot(h, wg_ref[0, :, cols].astype(BF16), preferred_element_type=F32)
        up = jnp.dot(h, wu_ref[0, :, cols].astype(BF16), preferred_element_type=F32)
        act_scr[:, cols] = (gate * jax.nn.sigmoid(gate) * up).astype(BF16)
    y = jnp.dot(act_scr[...], wd_ref[0].astype(BF16), preferred_element_type=F32)
    out = x + 0.5 * y
    if final:
        out = _rms(out, fg_ref[...])
    o_ref[...] = out


def _ffn(x2d, g, wg, wu, wd, fg, layer, *, final):
    t = x2d.shape[0]
    tile = pl.BlockSpec((TOK_TILE, D_MODEL), lambda i: (i, 0))
    return pl.pallas_call(
        functools.partial(_ffn_kernel, final=final),
        out_shape=jax.ShapeDtypeStruct((t, D_MODEL), F32),
        grid=(t // TOK_TILE,),
        in_specs=[tile, _const_spec((1, D_MODEL)), _layer_spec((D_MODEL, D_FF), layer),
                  _layer_spec((D_MODEL, D_FF), layer), _layer_spec((D_FF, D_MODEL), layer),
                  _const_spec((1, D_MODEL))],
        out_specs=tile,
        scratch_shapes=[pltpu.VMEM((TOK_TILE, D_MODEL), BF16), pltpu.VMEM((TOK_TILE, D_FF), BF16)],
        compiler_params=pltpu.CompilerParams(
            dimension_semantics=("arbitrary",), vmem_limit_bytes=VMEM_LIMIT),
        name="ffn_final" if final else "ffn",
    )(x2d, g, wg, wu, wd, fg)


def _memkv_kernel(mem_ref, g_ref, w_ref, k_ref, v_ref):
    mn = _rms(mem_ref[0], g_ref[0]).astype(BF16)
    kv = jnp.dot(mn, w_ref[0].astype(BF16), preferred_element_type=F32)
    k_ref[0, 0] = kv[:, :D_MODEL].astype(BF16)
    v_ref[0, 0] = kv[:, D_MODEL:].astype(BF16)


def _memkv(mem, g, w_kv):
    depth, batch = g.shape[0], mem.shape[0]
    out = jax.ShapeDtypeStruct((depth, batch, N_MEM, D_MODEL), BF16)
    out_spec = pl.BlockSpec((1, 1, N_MEM, D_MODEL), lambda l, b: (l, b, 0, 0))
    return pl.pallas_call(
        _memkv_kernel,
        out_shape=(out, out),
        grid=(depth, batch),
        in_specs=[pl.BlockSpec((1, N_MEM, D_MODEL), lambda l, b: (b, 0, 0)),
                  pl.BlockSpec((1, 1, D_MODEL), lambda l, b: (l, 0, 0)),
                  pl.BlockSpec((1, D_MODEL, 2 * D_MODEL), lambda l, b: (l, 0, 0))],
        out_specs=(out_spec, out_spec),
        compiler_params=pltpu.CompilerParams(
            dimension_semantics=("arbitrary", "arbitrary"), vmem_limit_bytes=VMEM_LIMIT),
        name="mem_kv",
    )(mem, g, w_kv)


def _log_sigmoid(z):
    return jnp.minimum(z, 0.0) - jnp.log1p(jnp.exp(-jnp.abs(z)))


def _split3(a):
    hi = a.astype(BF16).astype(F32)
    r = a - hi
    mid = r.astype(BF16).astype(F32)
    lo = (r - mid).astype(BF16).astype(F32)
    return hi, mid, lo


def _mix_in_kernel(x_ref, g_ref, w_ref, bf_ref, tri_ref, cw_ref, cb_ref, lng_ref,
                   lnb_ref, cog_ref, q_ref, k_ref, v_ref, c_ref, u_ref,
                   wqkv_scr, wf_scr, wag_scr, h_scr, ubuf, shifted, carry):
    s_idx = pl.program_id(1)

    @pl.when(jnp.logical_and(pl.program_id(0) == 0, s_idx == 0))
    def _():
        f0 = 3 * D_ATTN
        wqkv_scr[...] = w_ref[0, :, :f0].astype(BF16)
        wf_scr[...] = w_ref[0, :, f0:f0 + LANES].astype(BF16)
        wag_scr[...] = w_ref[0, :, f0 + N_ATTN_HEADS:].astype(BF16)

    @pl.when(s_idx == 0)
    def _():
        carry[...] = jnp.zeros_like(carry)
        ubuf[0:CONV_HALO, :] = jnp.zeros((CONV_HALO, D_CONV), F32)

    h_scr[...] = _rms(x_ref[0], g_ref[...]).astype(BF16)

    ag = jnp.dot(h_scr[...], wag_scr[...], preferred_element_type=F32)
    ubuf[CONV_HALO:, :] = ag[:, :D_CONV] * jax.nn.sigmoid(ag[:, D_CONV:])
    n_shift = TOK_TILE + CONV_HALO - SUBLANES
    for r in range(1, SUBLANES):
        shifted[r - 1] = ubuf[r:r + n_shift, :]

    qkv = jnp.dot(h_scr[...], wqkv_scr[...], preferred_element_type=F32)
    q_ref[0] = (qkv[:, :D_ATTN] * (LOG2E / math.sqrt(HEAD_DIM))).astype(BF16)
    k_ref[0] = qkv[:, D_ATTN:2 * D_ATTN].astype(BF16)
    v_ref[0] = qkv[:, 2 * D_ATTN:].astype(BF16)

    f_t = jnp.transpose(jnp.dot(h_scr[...], wf_scr[...], preferred_element_type=F32))
    logf = _log_sigmoid(f_t[:N_ATTN_HEADS, :] + bf_ref[...]) * LOG2E
    parts = jnp.concatenate(_split3(logf) + (jnp.zeros_like(logf),), axis=0).astype(BF16)
    sums = jnp.dot(parts, tri_ref[...], preferred_element_type=F32)
    h8 = N_ATTN_HEADS
    c_tile = carry[...] + sums[:h8] + sums[h8:2 * h8] + sums[2 * h8:3 * h8]
    c_ref[0] = c_tile
    carry[...] = c_tile[:, TOK_TILE - 1:TOK_TILE]

    for rb in range(TOK_TILE // CONV_ROWS):
        r0 = rb * CONV_ROWS
        acc = jnp.zeros((CONV_ROWS // SUBLANES, SUBLANES, D_CONV), F32)
        for tap in range(CONV_WIDTH):
            off = r0 + tap + CONV_HALO - (CONV_WIDTH - 1)
            sh, al = off % SUBLANES, (off // SUBLANES) * SUBLANES
            if sh == 0:
                blk = ubuf[al:al + CONV_ROWS, :]
            else:
                blk = shifted[sh - 1, al:al + CONV_ROWS, :]
            acc = acc + blk.reshape(CONV_ROWS // SUBLANES, SUBLANES, D_CONV) * cw_ref[tap]
        y = acc.reshape(CONV_ROWS, D_CONV) + cb_ref[...]
        mu = jnp.mean(y, axis=-1, keepdims=True)
        yc = y - mu
        yn = yc * lax.rsqrt(jnp.mean(yc * yc, axis=-1, keepdims=True) + EPS) * lng_ref[...] + lnb_ref[...]
        sw = yn * jax.nn.sigmoid(yn)
        u_ref[0, r0:r0 + CONV_ROWS, :] = _rms(sw, cog_ref[...]).astype(BF16)

    ubuf[0:CONV_HALO, :] = ubuf[TOK_TILE:TOK_TILE + CONV_HALO, :]


def _mix_in(x, g, w, bf, tri, cw, cb, lng, lnb, cog, layer):
    batch, seq, _ = x.shape
    tok = lambda w: pl.BlockSpec((1, TOK_TILE, w), lambda b, s: (b, s, 0))
    n_shift = TOK_TILE + CONV_HALO - SUBLANES
    outs = (jax.ShapeDtypeStruct((batch, seq, D_ATTN), BF16),) * 3 + (
        jax.ShapeDtypeStruct((batch, N_ATTN_HEADS, seq), F32),
        jax.ShapeDtypeStruct((batch, seq, D_CONV), BF16))
    return pl.pallas_call(
        _mix_in_kernel,
        out_shape=outs,
        grid=(batch, seq // TOK_TILE),
        in_specs=[tok(D_MODEL), _const_spec((1, D_MODEL)), _layer_spec((D_MODEL, D_IN), layer),
                  _const_spec((N_ATTN_HEADS, 1)), _const_spec((TOK_TILE, TOK_TILE)),
                  _const_spec((CONV_WIDTH, SUBLANES, D_CONV)),
                  _const_spec((1, D_CONV)), _const_spec((1, D_CONV)), _const_spec((1, D_CONV)),
                  _const_spec((1, D_CONV))],
        out_specs=(tok(D_ATTN), tok(D_ATTN), tok(D_ATTN),
                   pl.BlockSpec((1, N_ATTN_HEADS, TOK_TILE), lambda b, s: (b, 0, s)), tok(D_CONV)),
        scratch_shapes=[pltpu.VMEM((D_MODEL, 3 * D_ATTN), BF16), pltpu.VMEM((D_MODEL, LANES), BF16),
                        pltpu.VMEM((D_MODEL, 2 * D_CONV), BF16),
                        pltpu.VMEM((TOK_TILE, D_MODEL), BF16),
                        pltpu.VMEM((TOK_TILE + CONV_HALO, D_CONV), F32),
                        pltpu.VMEM((SUBLANES - 1, n_shift, D_CONV), F32),
                        pltpu.VMEM((N_ATTN_HEADS, 1), F32)],
        compiler_params=pltpu.CompilerParams(
            dimension_semantics=("arbitrary", "arbitrary"), vmem_limit_bytes=VMEM_LIMIT),
        name="mix_in",
    )(x, g, w, bf, tri, cw, cb, lng, lnb, cog)


def _fox_kernel(q_ref, k_ref, v_ref, ccol_ref, crow_ref, g_ref, o_ref,
                qm_scr, ct_scr, m_scr, l_scr, acc_scr, s_scr):
    i = pl.program_id(1)
    n_full = i // 2
    lane = lax.broadcasted_iota(jnp.int32, (ATT_TILE, LANES), 1)
    first = lane < HEAD_DIM
    row = lax.broadcasted_iota(jnp.int32, (ATT_TILE, LANES), 0)

    c_t = ccol_ref[0]
    for p in range(N_PAIRS):
        qp = q_ref[0, :, p * LANES:(p + 1) * LANES]
        zero = jnp.zeros_like(qp)
        qm_scr[2 * p] = jnp.where(first, qp, zero)
        qm_scr[2 * p + 1] = jnp.where(first, zero, qp)
    for h in range(N_ATTN_HEADS):
        ct_scr[h] = jnp.broadcast_to(c_t[:, h:h + 1], (ATT_TILE, LANES))
    m_scr[...] = jnp.full(m_scr.shape, NEG_INF, F32)
    l_scr[...] = jnp.zeros(l_scr.shape, F32)
    acc_scr[...] = jnp.zeros(acc_scr.shape, F32)

    def scores(t, h, n_chunk):
        k0 = pl.multiple_of(t * KEY_TILE, KEY_TILE)
        pair = slice((h // 2) * LANES, (h // 2 + 1) * LANES)
        c_s = crow_ref[0, t]
        s = lax.dot_general(qm_scr[h], k_ref[0, pl.ds(k0, n_chunk * LANES), pair],
                            (((1,), (1,)), ((), ())), preferred_element_type=F32)
        ct = ct_scr[h]
        for c in range(n_chunk):
            cols = slice(c * LANES, (c + 1) * LANES)
            s_scr[h, :, cols] = s[:, cols] + ct - c_s[h:h + 1, cols]

    def update(t, h, n_chunk, diag_from):
        k0 = pl.multiple_of(t * KEY_TILE, KEY_TILE)
        pair = slice((h // 2) * LANES, (h // 2 + 1) * LANES)

        def logits(c):
            sc = s_scr[h, :, c * LANES:(c + 1) * LANES]
            if c >= diag_from:
                sc = jnp.where(row >= lane + (c - diag_from) * LANES, sc, NEG_INF)
            return sc

        m_old = m_scr[h]
        m_blk = functools.reduce(jnp.maximum, [logits(c) for c in range(n_chunk)])
        m_new = jnp.maximum(m_old, jnp.max(m_blk, axis=-1, keepdims=True))
        alpha = jnp.exp2(m_old - m_new)
        ps = [jnp.exp2(logits(c) - m_new) for c in range(n_chunk)]
        l_scr[h] = alpha * l_scr[h] + functools.reduce(jnp.add, ps)
        m_scr[h] = m_new
        acc_scr[h] = alpha * acc_scr[h]
        p_bf = jnp.concatenate(ps, axis=1).astype(BF16)
        acc_scr[h] += jnp.dot(p_bf, v_ref[0, pl.ds(k0, n_chunk * LANES), pair],
                              preferred_element_type=F32)

    all_chunks = KEY_TILE // LANES
    half_chunks = ATT_TILE // LANES
    last = N_ATTN_HEADS - 1

    def full_window(t, _):
        for h in range(N_ATTN_HEADS):
            if h < last:
                scores(t, h + 1, all_chunks)
            else:
                scores(t + 1, 0, all_chunks)
            update(t, h, all_chunks, all_chunks)
        return 0

    scores(0, 0, all_chunks)
    lax.fori_loop(0, n_full, full_window, 0)

    @pl.when(i % 2 == 0)
    def _():
        for h in range(N_ATTN_HEADS):
            if h < last:
                scores(n_full, h + 1, half_chunks)
            update(n_full, h, half_chunks, 0)

    @pl.when(i % 2 == 1)
    def _():
        for h in range(N_ATTN_HEADS):
            if h < last:
                scores(n_full, h + 1, all_chunks)
            update(n_full, h, all_chunks, half_chunks)

    outs = []
    for p in range(N_PAIRS):
        l0 = jnp.sum(l_scr[2 * p], axis=-1, keepdims=True)
        l1 = jnp.sum(l_scr[2 * p + 1], axis=-1, keepdims=True)
        outs.append(jnp.where(first, acc_scr[2 * p] / l0, acc_scr[2 * p + 1] / l1))
    o_ref[0] = _rms(jnp.concatenate(outs, axis=1), g_ref[...]).astype(BF16)


def _fox_attention(q, k, v, ccol, crow, g):
    batch, seq, _ = q.shape
    n_win = seq // KEY_TILE
    stat = pltpu.VMEM((N_ATTN_HEADS, ATT_TILE, LANES), F32)
    return pl.pallas_call(
        _fox_kernel,
        out_shape=jax.ShapeDtypeStruct((batch, seq, D_ATTN), BF16),
        grid=(batch, seq // ATT_TILE),
        in_specs=[pl.BlockSpec((1, ATT_TILE, D_ATTN), lambda b, i: (b, i, 0)),
                  pl.BlockSpec((1, seq, D_ATTN), lambda b, i: (b, 0, 0)),
                  pl.BlockSpec((1, seq, D_ATTN), lambda b, i: (b, 0, 0)),
                  pl.BlockSpec((1, ATT_TILE, N_ATTN_HEADS), lambda b, i: (b, i, 0)),
                  pl.BlockSpec((1, n_win, N_ATTN_HEADS, KEY_TILE), lambda b, i: (b, 0, 0, 0)),
                  _const_spec((1, D_ATTN))],
        out_specs=pl.BlockSpec((1, ATT_TILE, D_ATTN), lambda b, i: (b, i, 0)),
        scratch_shapes=[pltpu.VMEM((N_ATTN_HEADS, ATT_TILE, LANES), BF16), stat, stat, stat, stat,
                        pltpu.VMEM((N_ATTN_HEADS, ATT_TILE, KEY_TILE), F32)],
        compiler_params=pltpu.CompilerParams(
            dimension_semantics=("arbitrary", "arbitrary"), vmem_limit_bytes=VMEM_LIMIT),
        name="fox_attn",
    )(q, k, v, ccol, crow, g)


def _mix_out_kernel(x_ref, a_ref, u_ref, wout_ref, xg_ref, wq_ref, mk_ref, mv_ref, wo_ref,
                    o_ref, o_scr):
    y = jnp.dot(a_ref[0], wout_ref[0, :D_ATTN, :].astype(BF16), preferred_element_type=F32)
    y = y + jnp.dot(u_ref[0], wout_ref[0, D_ATTN:, :].astype(BF16), preferred_element_type=F32)
    x = x_ref[0] + y

    hq = _rms(x, xg_ref[...]).astype(BF16)
    q = jnp.dot(hq, wq_ref[0].astype(BF16), preferred_element_type=F32) * (1.0 / math.sqrt(XATTN_HEAD_DIM))
    q = q.astype(BF16)
    for h in range(N_XATTN_HEADS):
        cols = slice(h * XATTN_HEAD_DIM, (h + 1) * XATTN_HEAD_DIM)
        s = lax.dot_general(q[:, cols], mk_ref[0, 0, :, cols], (((1,), (1,)), ((), ())),
                            preferred_element_type=F32)
        p = jnp.exp(s - jnp.max(s, axis=-1, keepdims=True))
        p = p / jnp.sum(p, axis=-1, keepdims=True)
        o_scr[:, cols] = jnp.dot(p.astype(BF16), mv_ref[0, 0, :, cols],
                                 preferred_element_type=F32).astype(BF16)
    o_ref[0] = x + jnp.dot(o_scr[...], wo_ref[0].astype(BF16), preferred_element_type=F32)


def _mix_out(x, attn, u, wout, xg, wq, mk, mv, wo, layer):
    batch, seq, _ = x.shape
    tok = lambda w: pl.BlockSpec((1, TOK_TILE, w), lambda b, s: (b, s, 0))
    mem_spec = pl.BlockSpec((1, 1, N_MEM, D_MODEL), lambda b, s: (layer, b, 0, 0))
    return pl.pallas_call(
        _mix_out_kernel,
        out_shape=jax.ShapeDtypeStruct((batch, seq, D_MODEL), F32),
        grid=(batch, seq // TOK_TILE),
        in_specs=[tok(D_MODEL), tok(D_ATTN), tok(D_CONV),
                  _layer_spec((D_MODEL, D_MODEL), layer), _const_spec((1, D_MODEL)),
                  _layer_spec((D_MODEL, D_MODEL), layer), mem_spec, mem_spec,
                  _layer_spec((D_MODEL, D_MODEL), layer)],
        out_specs=tok(D_MODEL),
        scratch_shapes=[pltpu.VMEM((TOK_TILE, D_MODEL), BF16)],
        compiler_params=pltpu.CompilerParams(
            dimension_semantics=("arbitrary", "arbitrary"), vmem_limit_bytes=VMEM_LIMIT),
        name="mix_out",
    )(x, attn, u, wout, xg, wq, mk, mv, wo)


def kernel(x, mem, ffn1_norm_g, ffn1_w_gate, ffn1_w_up, ffn1_w_down, mix_norm_g, w_in, b_f, conv_w, conv_b, conv_ln_g, conv_ln_b, attn_out_g, conv_out_g, w_out, xattn_norm_g, mem_norm_g, xattn_w_q, xattn_w_kv, xattn_w_o, ffn2_norm_g, ffn2_w_gate, ffn2_w_up, ffn2_w_down, final_norm_g):
    batch, seq, _ = x.shape
    depth = w_in.shape[0]
    n_win = seq // KEY_TILE
    row = lambda v: v.reshape(1, -1)

    mem_k, mem_v = _memkv(mem, mem_norm_g.reshape(depth, 1, D_MODEL), xattn_w_kv)
    fg = row(final_norm_g)
    tri = jnp.triu(jnp.ones((TOK_TILE, TOK_TILE), BF16))

    for l in range(depth):
        x2d = _ffn(x.reshape(batch * seq, D_MODEL), row(ffn1_norm_g[l]), ffn1_w_gate, ffn1_w_up,
                   ffn1_w_down, fg, l, final=False)
        x = x2d.reshape(batch, seq, D_MODEL)

        cw = jnp.broadcast_to(conv_w[l][:, None, :], (CONV_WIDTH, SUBLANES, D_CONV))
        q, k, v, c, u = _mix_in(
            x, row(mix_norm_g[l]), w_in, b_f[l].reshape(N_ATTN_HEADS, 1), tri, cw, row(conv_b[l]),
            row(conv_ln_g[l]), row(conv_ln_b[l]), row(conv_out_g[l]), l)

        ccol = c.transpose(0, 2, 1)
        crow = c.reshape(batch, N_ATTN_HEADS, n_win, KEY_TILE).transpose(0, 2, 1, 3)
        attn = _fox_attention(q, k, v, ccol, crow, row(attn_out_g[l]))

        x = _mix_out(x, attn, u, w_out, row(xattn_norm_g[l]), xattn_w_q, mem_k, mem_v, xattn_w_o, l)

        last = l == depth - 1
        x2d = _ffn(x.reshape(batch * seq, D_MODEL), row(ffn2_norm_g[l]), ffn2_w_gate, ffn2_w_up,
                   ffn2_w_down, fg, l, final=last)
        x = x2d.reshape(batch, seq, D_MODEL)
    return x
```

```python
import functools
import math

import jax
import jax.numpy as jnp
from jax import lax
from jax.experimental import pallas as pl
from jax.experimental.pallas import tpu as pltpu

F32 = jnp.float32
BF16 = jnp.bfloat16

D_MODEL = 1024
N_MEM = 256
D_ATTN = 512
D_CONV = 512
HEAD_DIM = 64
N_ATTN_HEADS = 8
CONV_WIDTH = 31
N_XATTN_HEADS = 4
XATTN_HEAD_DIM = 256
D_FF = 2816
D_IN = 3 * D_ATTN + N_ATTN_HEADS + 2 * D_CONV
EPS = 1e-6
NEG_INF = -1e30
LOG2E = math.log2(math.e)

LANES = 128
SUBLANES = 8
MXU_DIM = 256
VMEM_LIMIT = 56 * 1024 * 1024

TOK_TILE = 512
ATT_TILE = 256
KEY_TILE = 2 * ATT_TILE
FF_CHUNK = MXU_DIM
CONV_HALO = 32
CONV_ROWS = 64
N_PAIRS = D_ATTN // LANES


def _rms(x, g):
    return x * lax.rsqrt(jnp.mean(x * x, axis=-1, keepdims=True) + EPS) * g


def _const_spec(shape):
    nd = len(shape)
    return pl.BlockSpec(shape, lambda *_: (0,) * nd, pipeline_mode=pl.Buffered(1))


def _layer_spec(shape, layer):
    nd = len(shape)
    return pl.BlockSpec((1,) + shape, lambda *_: (layer,) + (0,) * nd, pipeline_mode=pl.Buffered(1))


def _ffn_kernel(x_ref, g_ref, wg_ref, wu_ref, wd_ref, fg_ref, o_ref, h_scr, act_scr, *, final):
    x = x_ref[...]
    h_scr[...] = _rms(x, g_ref[...]).astype(BF16)
    for c in range(D_FF // FF_CHUNK):
        cols = slice(c * FF_CHUNK, (c + 1) * FF_CHUNK)
        h = h_scr[...]
        gate = jnp.dot(h, wg_ref[0, :, cols].astype(BF16), preferred_element_type=F32)
        up = jnp.dot(h, wu_ref[0, :, cols].astype(BF16), preferred_element_type=F32)
        act_scr[:, cols] = (gate * jax.nn.sigmoid(gate) * up).astype(BF16)
    y = jnp.dot(act_scr[...], wd_ref[0].astype(BF16), preferred_element_type=F32)
    out = x + 0.5 * y
    if final:
        out = _rms(out, fg_ref[...])
    o_ref[...] = out


def _ffn(x2d, g, wg, wu, wd, fg, layer, *, final):
    t = x2d.shape[0]
    tile = pl.BlockSpec((TOK_TILE, D_MODEL), lambda i: (i, 0))
    return pl.pallas_call(
        functools.partial(_ffn_kernel, final=final),
        out_shape=jax.ShapeDtypeStruct((t, D_MODEL), F32),
        grid=(t // TOK_TILE,),
        in_specs=[tile, _const_spec((1, D_MODEL)), _layer_spec((D_MODEL, D_FF), layer),
                  _layer_spec((D_MODEL, D_FF), layer), _layer_spec((D_FF, D_MODEL), layer),
                  _const_spec((1, D_MODEL))],
        out_specs=tile,
        scratch_shapes=[pltpu.VMEM((TOK_TILE, D_MODEL), BF16), pltpu.VMEM((TOK_TILE, D_FF), BF16)],
        compiler_params=pltpu.CompilerParams(
            dimension_semantics=("arbitrary",), vmem_limit_bytes=VMEM_LIMIT),
        name="ffn_final" if final else "ffn",
    )(x2d, g, wg, wu, wd, fg)


def _memkv_kernel(mem_ref, g_ref, w_ref, k_ref, v_ref):
    mn = _rms(mem_ref[0], g_ref[0]).astype(BF16)
    kv = jnp.dot(mn, w_ref[0].astype(BF16), preferred_element_type=F32)
    k_ref[0, 0] = kv[:, :D_MODEL].astype(BF16)
    v_ref[0, 0] = kv[:, D_MODEL:].astype(BF16)


def _memkv(mem, g, w_kv):
    depth, batch = g.shape[0], mem.shape[0]
    out = jax.ShapeDtypeStruct((depth, batch, N_MEM, D_MODEL), BF16)
    out_spec = pl.BlockSpec((1, 1, N_MEM, D_MODEL), lambda l, b: (l, b, 0, 0))
    return pl.pallas_call(
        _memkv_kernel,
        out_shape=(out, out),
        grid=(depth, batch),
        in_specs=[pl.BlockSpec((1, N_MEM, D_MODEL), lambda l, b: (b, 0, 0)),
                  pl.BlockSpec((1, 1, D_MODEL), lambda l, b: (l, 0, 0)),
                  pl.BlockSpec((1, D_MODEL, 2 * D_MODEL), lambda l, b: (l, 0, 0))],
        out_specs=(out_spec, out_spec),
        compiler_params=pltpu.CompilerParams(
            dimension_semantics=("arbitrary", "arbitrary"), vmem_limit_bytes=VMEM_LIMIT),
        name="mem_kv",
    )(mem, g, w_kv)


def _log_sigmoid(z):
    return jnp.minimum(z, 0.0) - jnp.log1p(jnp.exp(-jnp.abs(z)))


def _split3(a):
    hi = a.astype(BF16).astype(F32)
    r = a - hi
    mid = r.astype(BF16).astype(F32)
    lo = (r - mid).astype(BF16).astype(F32)
    return hi, mid, lo


def _mix_in_kernel(x_ref, g_ref, w_ref, bf_ref, tri_ref, cw_ref, cb_ref, lng_ref,
                   lnb_ref, cog_ref, q_ref, k_ref, v_ref, c_ref, u_ref,
                   wqkv_scr, wf_scr, wag_scr, h_scr, ubuf, shifted, carry):
    s_idx = pl.program_id(1)

    @pl.when(jnp.logical_and(pl.program_id(0) == 0, s_idx == 0))
    def _():
        f0 = 3 * D_ATTN
        wqkv_scr[...] = jnp.transpose(w_ref[0, :f0, :]).astype(BF16)
        wf_scr[...] = jnp.transpose(w_ref[0, f0:f0 + LANES, :]).astype(BF16)
        wag_scr[...] = jnp.transpose(w_ref[0, f0 + N_ATTN_HEADS:, :]).astype(BF16)

    def project(w_scr):
        return jnp.dot(h_scr[...], w_scr[...], preferred_element_type=F32)

    @pl.when(s_idx == 0)
    def _():
        carry[...] = jnp.zeros_like(carry)
        ubuf[0:CONV_HALO, :] = jnp.zeros((CONV_HALO, D_CONV), F32)

    h_scr[...] = _rms(x_ref[0], g_ref[...]).astype(BF16)

    ag = project(wag_scr)
    ubuf[CONV_HALO:, :] = ag[:, :D_CONV] * jax.nn.sigmoid(ag[:, D_CONV:])
    n_shift = TOK_TILE + CONV_HALO - SUBLANES
    for r in range(1, SUBLANES):
        shifted[r - 1] = ubuf[r:r + n_shift, :]

    qkv = project(wqkv_scr)
    q_ref[0] = (qkv[:, :D_ATTN] * (LOG2E / math.sqrt(HEAD_DIM))).astype(BF16)
    k_ref[0] = qkv[:, D_ATTN:2 * D_ATTN].astype(BF16)
    v_ref[0] = qkv[:, 2 * D_ATTN:].astype(BF16)

    f_t = jnp.transpose(project(wf_scr))
    logf = _log_sigmoid(f_t[:N_ATTN_HEADS, :] + bf_ref[...]) * LOG2E
    parts = jnp.concatenate(_split3(logf) + (jnp.zeros_like(logf),), axis=0).astype(BF16)
    sums = jnp.dot(parts, tri_ref[...], preferred_element_type=F32)
    h8 = N_ATTN_HEADS
    c_tile = carry[...] + sums[:h8] + sums[h8:2 * h8] + sums[2 * h8:3 * h8]
    c_ref[0] = c_tile
    carry[...] = c_tile[:, TOK_TILE - 1:TOK_TILE]

    for rb in range(TOK_TILE // CONV_ROWS):
        r0 = rb * CONV_ROWS
        acc = jnp.zeros((CONV_ROWS // SUBLANES, SUBLANES, D_CONV), F32)
        for tap in range(CONV_WIDTH):
            off = r0 + tap + CONV_HALO - (CONV_WIDTH - 1)
            sh, al = off % SUBLANES, (off // SUBLANES) * SUBLANES
            if sh == 0:
                blk = ubuf[al:al + CONV_ROWS, :]
            else:
                blk = shifted[sh - 1, al:al + CONV_ROWS, :]
            acc = acc + blk.reshape(CONV_ROWS // SUBLANES, SUBLANES, D_CONV) * cw_ref[tap]
        y = acc.reshape(CONV_ROWS, D_CONV) + cb_ref[...]
        mu = jnp.mean(y, axis=-1, keepdims=True)
        yc = y - mu
        yn = yc * lax.rsqrt(jnp.mean(yc * yc, axis=-1, keepdims=True) + EPS) * lng_ref[...] + lnb_ref[...]
        sw = yn * jax.nn.sigmoid(yn)
        u_ref[0, r0:r0 + CONV_ROWS, :] = _rms(sw, cog_ref[...]).astype(BF16)

    ubuf[0:CONV_HALO, :] = ubuf[TOK_TILE:TOK_TILE + CONV_HALO, :]


def _mix_in(x, g, w, bf, tri, cw, cb, lng, lnb, cog, layer):
    batch, seq, _ = x.shape
    tok = lambda w: pl.BlockSpec((1, TOK_TILE, w), lambda b, s: (b, s, 0))
    n_shift = TOK_TILE + CONV_HALO - SUBLANES
    outs = (jax.ShapeDtypeStruct((batch, seq, D_ATTN), BF16),) * 3 + (
        jax.ShapeDtypeStruct((batch, N_ATTN_HEADS, seq), F32),
        jax.ShapeDtypeStruct((batch, seq, D_CONV), BF16))
    return pl.pallas_call(
        _mix_in_kernel,
        out_shape=outs,
        grid=(batch, seq // TOK_TILE),
        in_specs=[tok(D_MODEL), _const_spec((1, D_MODEL)), _layer_spec((D_IN, D_MODEL), layer),
                  _const_spec((N_ATTN_HEADS, 1)), _const_spec((TOK_TILE, TOK_TILE)),
                  _const_spec((CONV_WIDTH, SUBLANES, D_CONV)),
                  _const_spec((1, D_CONV)), _const_spec((1, D_CONV)), _const_spec((1, D_CONV)),
                  _const_spec((1, D_CONV))],
        out_specs=(tok(D_ATTN), tok(D_ATTN), tok(D_ATTN),
                   pl.BlockSpec((1, N_ATTN_HEADS, TOK_TILE), lambda b, s: (b, 0, s)), tok(D_CONV)),
        scratch_shapes=[pltpu.VMEM((D_MODEL, 3 * D_ATTN), BF16), pltpu.VMEM((D_MODEL, LANES), BF16),
                        pltpu.VMEM((D_MODEL, 2 * D_CONV), BF16),
                        pltpu.VMEM((TOK_TILE, D_MODEL), BF16),
                        pltpu.VMEM((TOK_TILE + CONV_HALO, D_CONV), F32),
                        pltpu.VMEM((SUBLANES - 1, n_shift, D_CONV), F32),
                        pltpu.VMEM((N_ATTN_HEADS, 1), F32)],
        compiler_params=pltpu.CompilerParams(
            dimension_semantics=("arbitrary", "arbitrary"), vmem_limit_bytes=VMEM_LIMIT),
        name="mix_in",
    )(x, g, w, bf, tri, cw, cb, lng, lnb, cog)


def _fox_kernel(q_ref, k_ref, v_ref, ccol_ref, crow_ref, g_ref, o_ref,
                qm_scr, ct_scr, m_scr, l_scr, acc_scr, s_scr):
    i = pl.program_id(1)
    n_full = i // 2
    lane = lax.broadcasted_iota(jnp.int32, (ATT_TILE, LANES), 1)
    first = lane < HEAD_DIM
    row = lax.broadcasted_iota(jnp.int32, (ATT_TILE, LANES), 0)

    c_t = ccol_ref[0]
    for p in range(N_PAIRS):
        qp = q_ref[0, :, p * LANES:(p + 1) * LANES]
        zero = jnp.zeros_like(qp)
        qm_scr[2 * p] = jnp.where(first, qp, zero)
        qm_scr[2 * p + 1] = jnp.where(first, zero, qp)
    for h in range(N_ATTN_HEADS):
        ct_scr[h] = jnp.broadcast_to(c_t[:, h:h + 1], (ATT_TILE, LANES))
    m_scr[...] = jnp.full(m_scr.shape, NEG_INF, F32)
    l_scr[...] = jnp.zeros(l_scr.shape, F32)
    acc_scr[...] = jnp.zeros(acc_scr.shape, F32)

    def scores(t, h, n_chunk):
        k0 = pl.multiple_of(t * KEY_TILE, KEY_TILE)
        pair = slice((h // 2) * LANES, (h // 2 + 1) * LANES)
        c_s = crow_ref[0, t]
        s = lax.dot_general(qm_scr[h], k_ref[0, pl.ds(k0, n_chunk * LANES), pair],
                            (((1,), (1,)), ((), ())), preferred_element_type=F32)
        ct = ct_scr[h]
        for c in range(n_chunk):
            cols = slice(c * LANES, (c + 1) * LANES)
            s_scr[h, :, cols] = s[:, cols] + ct - c_s[h:h + 1, cols]

    def update(t, h, n_chunk, diag_from):
        k0 = pl.multiple_of(t * KEY_TILE, KEY_TILE)
        pair = slice((h // 2) * LANES, (h // 2 + 1) * LANES)

        def logits(c):
            sc = s_scr[h, :, c * LANES:(c + 1) * LANES]
            if c >= diag_from:
                sc = jnp.where(row >= lane + (c - diag_from) * LANES, sc, NEG_INF)
            return sc

        m_old = m_scr[h]
        m_blk = functools.reduce(jnp.maximum, [logits(c) for c in range(n_chunk)])
        m_new = jnp.maximum(m_old, jnp.max(m_blk, axis=-1, keepdims=True))
        alpha = jnp.exp2(m_old - m_new)
        ps = [jnp.exp2(logits(c) - m_new) for c in range(n_chunk)]
        l_scr[h] = alpha * l_scr[h] + functools.reduce(jnp.add, ps)
        m_scr[h] = m_new
        acc_scr[h] = alpha * acc_scr[h]
        p_bf = jnp.concatenate(ps, axis=1).astype(BF16)
        acc_scr[h] += jnp.dot(p_bf, v_ref[0, pl.ds(k0, n_chunk * LANES), pair],
                              preferred_element_type=F32)

    all_chunks = KEY_TILE // LANES
    half_chunks = ATT_TILE // LANES
    last = N_ATTN_HEADS - 1

    def full_window(t, _):
        for h in range(N_ATTN_HEADS):
            if h < last:
                scores(t, h + 1, all_chunks)
            else:
                scores(t + 1, 0, all_chunks)
            update(t, h, all_chunks, all_chunks)
        return 0

    scores(0, 0, all_chunks)
    lax.fori_loop(0, n_full, full_window, 0)

    @pl.when(i % 2 == 0)
    def _():
        for h in range(N_ATTN_HEADS):
            if h < last:
                scores(n_full, h + 1, half_chunks)
            update(n_full, h, half_chunks, 0)

    @pl.when(i % 2 == 1)
    def _():
        for h in range(N_ATTN_HEADS):
            if h < last:
                scores(n_full, h + 1, all_chunks)
            update(n_full, h, all_chunks, half_chunks)

    outs = []
    for p in range(N_PAIRS):
        l0 = jnp.sum(l_scr[2 * p], axis=-1, keepdims=True)
        l1 = jnp.sum(l_scr[2 * p + 1], axis=-1, keepdims=True)
        outs.append(jnp.where(first, acc_scr[2 * p] / l0, acc_scr[2 * p + 1] / l1))
    o_ref[0] = _rms(jnp.concatenate(outs, axis=1), g_ref[...]).astype(BF16)


def _fox_attention(q, k, v, ccol, crow, g):
    batch, seq, _ = q.shape
    n_win = seq // KEY_TILE
    stat = pltpu.VMEM((N_ATTN_HEADS, ATT_TILE, LANES), F32)
    return pl.pallas_call(
        _fox_kernel,
        out_shape=jax.ShapeDtypeStruct((batch, seq, D_ATTN), BF16),
        grid=(batch, seq // ATT_TILE),
        in_specs=[pl.BlockSpec((1, ATT_TILE, D_ATTN), lambda b, i: (b, i, 0)),
                  pl.BlockSpec((1, seq, D_ATTN), lambda b, i: (b, 0, 0)),
                  pl.BlockSpec((1, seq, D_ATTN), lambda b, i: (b, 0, 0)),
                  pl.BlockSpec((1, ATT_TILE, N_ATTN_HEADS), lambda b, i: (b, i, 0)),
                  pl.BlockSpec((1, n_win, N_ATTN_HEADS, KEY_TILE), lambda b, i: (b, 0, 0, 0)),
                  _const_spec((1, D_ATTN))],
        out_specs=pl.BlockSpec((1, ATT_TILE, D_ATTN), lambda b, i: (b, i, 0)),
        scratch_shapes=[pltpu.VMEM((N_ATTN_HEADS, ATT_TILE, LANES), BF16), stat, stat, stat, stat,
                        pltpu.VMEM((N_ATTN_HEADS, ATT_TILE, KEY_TILE), F32)],
        compiler_params=pltpu.CompilerParams(
            dimension_semantics=("arbitrary", "arbitrary"), vmem_limit_bytes=VMEM_LIMIT),
        name="fox_attn",
    )(q, k, v, ccol, crow, g)


def _mix_out_kernel(x_ref, a_ref, u_ref, wout_ref, xg_ref, wq_ref, mk_ref, mv_ref, wo_ref,
                    o_ref, o_scr):
    y = jnp.dot(a_ref[0], wout_ref[0, :D_ATTN, :].astype(BF16), preferred_element_type=F32)
    y = y + jnp.dot(u_ref[0], wout_ref[0, D_ATTN:, :].astype(BF16), preferred_element_type=F32)
    x = x_ref[0] + y

    hq = _rms(x, xg_ref[...]).astype(BF16)
    q = jnp.dot(hq, wq_ref[0].astype(BF16), preferred_element_type=F32) * (1.0 / math.sqrt(XATTN_HEAD_DIM))
    q = q.astype(BF16)
    for h in range(N_XATTN_HEADS):
        cols = slice(h * XATTN_HEAD_DIM, (h + 1) * XATTN_HEAD_DIM)
        s = lax.dot_general(q[:, cols], mk_ref[0, 0, :, cols], (((1,), (1,)), ((), ())),
                            preferred_element_type=F32)
        p = jnp.exp(s - jnp.max(s, axis=-1, keepdims=True))
        p = p / jnp.sum(p, axis=-1, keepdims=True)
        o_scr[:, cols] = jnp.dot(p.astype(BF16), mv_ref[0, 0, :, cols],
                                 preferred_element_type=F32).astype(BF16)
    o_ref[0] = x + jnp.dot(o_scr[...], wo_ref[0].astype(BF16), preferred_element_type=F32)


def _mix_out(x, attn, u, wout, xg, wq, mk, mv, wo, layer):
    batch, seq, _ = x.shape
    tok = lambda w: pl.BlockSpec((1, TOK_TILE, w), lambda b, s: (b, s, 0))
    mem_spec = pl.BlockSpec((1, 1, N_MEM, D_MODEL), lambda b, s: (layer, b, 0, 0))
    return pl.pallas_call(
        _mix_out_kernel,
        out_shape=jax.ShapeDtypeStruct((batch, seq, D_MODEL), F32),
        grid=(batch, seq // TOK_TILE),
        in_specs=[tok(D_MODEL), tok(D_ATTN), tok(D_CONV),
                  _layer_spec((D_MODEL, D_MODEL), layer), _const_spec((1, D_MODEL)),
                  _layer_spec((D_MODEL, D_MODEL), layer), mem_spec, mem_spec,
                  _layer_spec((D_MODEL, D_MODEL), layer)],
        out_specs=tok(D_MODEL),
        scratch_shapes=[pltpu.VMEM((TOK_TILE, D_MODEL), BF16)],
        compiler_params=pltpu.CompilerParams(
            dimension_semantics=("arbitrary", "arbitrary"), vmem_limit_bytes=VMEM_LIMIT),
        name="mix_out",
    )(x, attn, u, wout, xg, wq, mk, mv, wo)


def kernel(x, mem, ffn1_norm_g, ffn1_w_gate, ffn1_w_up, ffn1_w_down, mix_norm_g, w_in, b_f, conv_w, conv_b, conv_ln_g, conv_ln_b, attn_out_g, conv_out_g, w_out, xattn_norm_g, mem_norm_g, xattn_w_q, xattn_w_kv, xattn_w_o, ffn2_norm_g, ffn2_w_gate, ffn2_w_up, ffn2_w_down, final_norm_g):
    batch, seq, _ = x.shape
    depth = w_in.shape[0]
    n_win = seq // KEY_TILE
    row = lambda v: v.reshape(1, -1)

    mem_k, mem_v = _memkv(mem, mem_norm_g.reshape(depth, 1, D_MODEL), xattn_w_kv)
    fg = row(final_norm_g)
    tri = jnp.triu(jnp.ones((TOK_TILE, TOK_TILE), BF16))
    w_in_t = jnp.swapaxes(w_in, 1, 2)

    for l in range(depth):
        x2d = _ffn(x.reshape(batch * seq, D_MODEL), row(ffn1_norm_g[l]), ffn1_w_gate, ffn1_w_up,
                   ffn1_w_down, fg, l, final=False)
        x = x2d.reshape(batch, seq, D_MODEL)

        cw = jnp.broadcast_to(conv_w[l][:, None, :], (CONV_WIDTH, SUBLANES, D_CONV))
        q, k, v, c, u = _mix_in(
            x, row(mix_norm_g[l]), w_in_t, b_f[l].reshape(N_ATTN_HEADS, 1), tri, cw, row(conv_b[l]),
            row(conv_ln_g[l]), row(conv_ln_b[l]), row(conv_out_g[l]), l)

        ccol = c.transpose(0, 2, 1)
        crow = c.reshape(batch, N_ATTN_HEADS, n_win, KEY_TILE).transpose(0, 2, 1, 3)
        attn = _fox_attention(q, k, v, ccol, crow, row(attn_out_g[l]))

        x = _mix_out(x, attn, u, w_out, row(xattn_norm_g[l]), xattn_w_q, mem_k, mem_v, xattn_w_o, l)

        last = l == depth - 1
        x2d = _ffn(x.reshape(batch * seq, D_MODEL), row(ffn2_norm_g[l]), ffn2_w_gate, ffn2_w_up,
                   ffn2_w_down, fg, l, final=last)
        x = x2d.reshape(batch, seq, D_MODEL)
    return x
```

```python
import functools
import math

import jax
import jax.numpy as jnp
from jax import lax
from jax.experimental import pallas as pl
from jax.experimental.pallas import tpu as pltpu

F32 = jnp.float32
BF16 = jnp.bfloat16

D_MODEL = 1024
N_MEM = 256
D_ATTN = 512
D_CONV = 512
HEAD_DIM = 64
N_ATTN_HEADS = 8
CONV_WIDTH = 31
N_XATTN_HEADS = 4
XATTN_HEAD_DIM = 256
D_FF = 2816
D_IN = 3 * D_ATTN + N_ATTN_HEADS + 2 * D_CONV
EPS = 1e-6
NEG_INF = -1e30
LOG2E = math.log2(math.e)

LANES = 128
SUBLANES = 8
MXU_DIM = 256
VMEM_LIMIT = 56 * 1024 * 1024

TOK_TILE = 512
ATT_TILE = 256
KEY_TILE = 2 * ATT_TILE
FF_CHUNK = MXU_DIM
CONV_HALO = 32
CONV_ROWS = 64
OUT_TILE = 1024
N_PAIRS = D_ATTN // LANES


def _rms(x, g):
    return x * lax.rsqrt(jnp.mean(x * x, axis=-1, keepdims=True) + EPS) * g


def _const_spec(shape):
    nd = len(shape)
    return pl.BlockSpec(shape, lambda *_: (0,) * nd, pipeline_mode=pl.Buffered(1))


def _layer_spec(shape, layer):
    nd = len(shape)
    return pl.BlockSpec((1,) + shape, lambda *_: (layer,) + (0,) * nd, pipeline_mode=pl.Buffered(1))


def _ffn_kernel(x_ref, g_ref, wg_ref, wu_ref, wd_ref, fg_ref, o_ref, h_scr, act_scr, *, final):
    x = x_ref[...]
    h_scr[...] = _rms(x, g_ref[...]).astype(BF16)
    for c in range(D_FF // FF_CHUNK):
        cols = slice(c * FF_CHUNK, (c + 1) * FF_CHUNK)
        h = h_scr[...]
        gate = jnp.dot(h, wg_ref[0, :, cols].astype(BF16), preferred_element_type=F32)
        up = jnp.dot(h, wu_ref[0, :, cols].astype(BF16), preferred_element_type=F32)
        act_scr[:, cols] = (gate * jax.nn.sigmoid(gate) * up).astype(BF16)
    y = jnp.dot(act_scr[...], wd_ref[0].astype(BF16), preferred_element_type=F32)
    out = x + 0.5 * y
    if final:
        out = _rms(out, fg_ref[...])
    o_ref[...] = out


def _ffn(x2d, g, wg, wu, wd, fg, layer, *, final):
    t = x2d.shape[0]
    tile = pl.BlockSpec((TOK_TILE, D_MODEL), lambda i: (i, 0))
    return pl.pallas_call(
        functools.partial(_ffn_kernel, final=final),
        out_shape=jax.ShapeDtypeStruct((t, D_MODEL), F32),
        grid=(t // TOK_TILE,),
        in_specs=[tile, _const_spec((1, D_MODEL)), _layer_spec((D_MODEL, D_FF), layer),
                  _layer_spec((D_MODEL, D_FF), layer), _layer_spec((D_FF, D_MODEL), layer),
                  _const_spec((1, D_MODEL))],
        out_specs=tile,
        scratch_shapes=[pltpu.VMEM((TOK_TILE, D_MODEL), BF16), pltpu.VMEM((TOK_TILE, D_FF), BF16)],
        compiler_params=pltpu.CompilerParams(
            dimension_semantics=("arbitrary",), vmem_limit_bytes=VMEM_LIMIT),
        name="ffn_final" if final else "ffn",
    )(x2d, g, wg, wu, wd, fg)


def _memkv_kernel(mem_ref, g_ref, w_ref, k_ref, v_ref):
    mn = _rms(mem_ref[0], g_ref[0]).astype(BF16)
    kv = jnp.dot(mn, w_ref[0].astype(BF16), preferred_element_type=F32)
    k_ref[0, 0] = kv[:, :D_MODEL].astype(BF16)
    v_ref[0, 0] = kv[:, D_MODEL:].astype(BF16)


def _memkv(mem, g, w_kv):
    depth, batch = g.shape[0], mem.shape[0]
    out = jax.ShapeDtypeStruct((depth, batch, N_MEM, D_MODEL), BF16)
    out_spec = pl.BlockSpec((1, 1, N_MEM, D_MODEL), lambda l, b: (l, b, 0, 0))
    return pl.pallas_call(
        _memkv_kernel,
        out_shape=(out, out),
        grid=(depth, batch),
        in_specs=[pl.BlockSpec((1, N_MEM, D_MODEL), lambda l, b: (b, 0, 0)),
                  pl.BlockSpec((1, 1, D_MODEL), lambda l, b: (l, 0, 0)),
                  pl.BlockSpec((1, D_MODEL, 2 * D_MODEL), lambda l, b: (l, 0, 0))],
        out_specs=(out_spec, out_spec),
        compiler_params=pltpu.CompilerParams(
            dimension_semantics=("arbitrary", "arbitrary"), vmem_limit_bytes=VMEM_LIMIT),
        name="mem_kv",
    )(mem, g, w_kv)


def _log_sigmoid(z):
    return jnp.minimum(z, 0.0) - jnp.log1p(jnp.exp(-jnp.abs(z)))


def _split3(a):
    hi = a.astype(BF16).astype(F32)
    r = a - hi
    mid = r.astype(BF16).astype(F32)
    lo = (r - mid).astype(BF16).astype(F32)
    return hi, mid, lo


def _mix_in_kernel(x_ref, g_ref, w_ref, bf_ref, tri_ref, cw_ref, cb_ref, lng_ref,
                   lnb_ref, cog_ref, q_ref, k_ref, v_ref, c_ref, u_ref,
                   wqkv_scr, wf_scr, wag_scr, h_scr, ubuf, shifted, carry):
    s_idx = pl.program_id(1)

    @pl.when(jnp.logical_and(pl.program_id(0) == 0, s_idx == 0))
    def _():
        f0 = 3 * D_ATTN
        wqkv_scr[...] = jnp.transpose(w_ref[0, :f0, :]).astype(BF16)
        wf_scr[...] = jnp.transpose(w_ref[0, f0:f0 + LANES, :]).astype(BF16)
        wag_scr[...] = jnp.transpose(w_ref[0, f0 + N_ATTN_HEADS:, :]).astype(BF16)

    def project(w_scr):
        return jnp.dot(h_scr[...], w_scr[...], preferred_element_type=F32)

    @pl.when(s_idx == 0)
    def _():
        carry[...] = jnp.zeros_like(carry)
        ubuf[0:CONV_HALO, :] = jnp.zeros((CONV_HALO, D_CONV), F32)

    h_scr[...] = _rms(x_ref[0], g_ref[...]).astype(BF16)

    ag = project(wag_scr)
    ubuf[CONV_HALO:, :] = ag[:, :D_CONV] * jax.nn.sigmoid(ag[:, D_CONV:])
    n_shift = TOK_TILE + CONV_HALO - SUBLANES
    for r in range(1, SUBLANES):
        shifted[r - 1] = ubuf[r:r + n_shift, :]

    qkv = project(wqkv_scr)
    q_ref[0] = (qkv[:, :D_ATTN] * (LOG2E / math.sqrt(HEAD_DIM))).astype(BF16)
    k_ref[0] = qkv[:, D_ATTN:2 * D_ATTN].astype(BF16)
    ones = jnp.ones((TOK_TILE, LANES), BF16)
    for p in range(N_PAIRS):
        v_ref[0, :, 2 * p * LANES:(2 * p + 1) * LANES] = (
            qkv[:, 2 * D_ATTN + p * LANES:2 * D_ATTN + (p + 1) * LANES].astype(BF16))
        v_ref[0, :, (2 * p + 1) * LANES:(2 * p + 2) * LANES] = ones

    f_t = jnp.transpose(project(wf_scr))
    logf = _log_sigmoid(f_t[:N_ATTN_HEADS, :] + bf_ref[...]) * LOG2E
    parts = jnp.concatenate(_split3(logf) + (jnp.zeros_like(logf),), axis=0).astype(BF16)
    sums = jnp.dot(parts, tri_ref[...], preferred_element_type=F32)
    h8 = N_ATTN_HEADS
    c_tile = carry[...] + sums[:h8] + sums[h8:2 * h8] + sums[2 * h8:3 * h8]
    c_ref[0] = c_tile
    carry[...] = c_tile[:, TOK_TILE - 1:TOK_TILE]

    for rb in range(TOK_TILE // CONV_ROWS):
        r0 = rb * CONV_ROWS
        acc = jnp.zeros((CONV_ROWS // SUBLANES, SUBLANES, D_CONV), F32)
        for tap in range(CONV_WIDTH):
            off = r0 + tap + CONV_HALO - (CONV_WIDTH - 1)
            sh, al = off % SUBLANES, (off // SUBLANES) * SUBLANES
            if sh == 0:
                blk = ubuf[al:al + CONV_ROWS, :]
            else:
                blk = shifted[sh - 1, al:al + CONV_ROWS, :]
            acc = acc + blk.reshape(CONV_ROWS // SUBLANES, SUBLANES, D_CONV) * cw_ref[tap]
        y = acc.reshape(CONV_ROWS, D_CONV) + cb_ref[...]
        mu = jnp.mean(y, axis=-1, keepdims=True)
        yc = y - mu
        yn = yc * lax.rsqrt(jnp.mean(yc * yc, axis=-1, keepdims=True) + EPS) * lng_ref[...] + lnb_ref[...]
        sw = yn * jax.nn.sigmoid(yn)
        u_ref[0, r0:r0 + CONV_ROWS, :] = _rms(sw, cog_ref[...]).astype(BF16)

    ubuf[0:CONV_HALO, :] = ubuf[TOK_TILE:TOK_TILE + CONV_HALO, :]


def _mix_in(x, g, w, bf, tri, cw, cb, lng, lnb, cog, layer):
    batch, seq, _ = x.shape
    tok = lambda w: pl.BlockSpec((1, TOK_TILE, w), lambda b, s: (b, s, 0))
    n_shift = TOK_TILE + CONV_HALO - SUBLANES
    outs = (jax.ShapeDtypeStruct((batch, seq, D_ATTN), BF16),) * 2 + (
        jax.ShapeDtypeStruct((batch, seq, 2 * D_ATTN), BF16),
        jax.ShapeDtypeStruct((batch, N_ATTN_HEADS, seq), F32),
        jax.ShapeDtypeStruct((batch, seq, D_CONV), BF16))
    return pl.pallas_call(
        _mix_in_kernel,
        out_shape=outs,
        grid=(batch, seq // TOK_TILE),
        in_specs=[tok(D_MODEL), _const_spec((1, D_MODEL)), _layer_spec((D_IN, D_MODEL), layer),
                  _const_spec((N_ATTN_HEADS, 1)), _const_spec((TOK_TILE, TOK_TILE)),
                  _const_spec((CONV_WIDTH, SUBLANES, D_CONV)),
                  _const_spec((1, D_CONV)), _const_spec((1, D_CONV)), _const_spec((1, D_CONV)),
                  _const_spec((1, D_CONV))],
        out_specs=(tok(D_ATTN), tok(D_ATTN), tok(2 * D_ATTN),
                   pl.BlockSpec((1, N_ATTN_HEADS, TOK_TILE), lambda b, s: (b, 0, s)), tok(D_CONV)),
        scratch_shapes=[pltpu.VMEM((D_MODEL, 3 * D_ATTN), BF16), pltpu.VMEM((D_MODEL, LANES), BF16),
                        pltpu.VMEM((D_MODEL, 2 * D_CONV), BF16),
                        pltpu.VMEM((TOK_TILE, D_MODEL), BF16),
                        pltpu.VMEM((TOK_TILE + CONV_HALO, D_CONV), F32),
                        pltpu.VMEM((SUBLANES - 1, n_shift, D_CONV), F32),
                        pltpu.VMEM((N_ATTN_HEADS, 1), F32)],
        compiler_params=pltpu.CompilerParams(
            dimension_semantics=("arbitrary", "arbitrary"), vmem_limit_bytes=VMEM_LIMIT),
        name="mix_in",
    )(x, g, w, bf, tri, cw, cb, lng, lnb, cog)


def _fox_kernel(q_ref, k_ref, v_ref, ccol_ref, crow_ref, g_ref, o_ref,
                qm_scr, ct_scr, m_scr, acc_scr, s_scr):
    i = pl.program_id(1)
    n_full = i // 2
    lane = lax.broadcasted_iota(jnp.int32, (ATT_TILE, LANES), 1)
    first = lane < HEAD_DIM
    row = lax.broadcasted_iota(jnp.int32, (ATT_TILE, LANES), 0)

    c_t = ccol_ref[0]
    for p in range(N_PAIRS):
        qp = q_ref[0, :, p * LANES:(p + 1) * LANES]
        zero = jnp.zeros_like(qp)
        qm_scr[2 * p] = jnp.where(first, qp, zero)
        qm_scr[2 * p + 1] = jnp.where(first, zero, qp)
    for h in range(N_ATTN_HEADS):
        ct_scr[h] = jnp.broadcast_to(c_t[:, h:h + 1], (ATT_TILE, LANES))
    m_scr[...] = jnp.full(m_scr.shape, NEG_INF, F32)
    acc_scr[...] = jnp.zeros(acc_scr.shape, F32)

    def scores(t, h, n_chunk):
        k0 = pl.multiple_of(t * KEY_TILE, KEY_TILE)
        pair = slice((h // 2) * LANES, (h // 2 + 1) * LANES)
        c_s = crow_ref[0, t]
        s = lax.dot_general(qm_scr[h], k_ref[0, pl.ds(k0, n_chunk * LANES), pair],
                            (((1,), (1,)), ((), ())), preferred_element_type=F32)
        ct = ct_scr[h]
        for c in range(n_chunk):
            cols = slice(c * LANES, (c + 1) * LANES)
            s_scr[h, :, cols] = s[:, cols] + ct - c_s[h:h + 1, cols]

    def update(t, h, n_chunk, diag_from):
        k0 = pl.multiple_of(t * KEY_TILE, KEY_TILE)
        vcols = slice((h // 2) * 2 * LANES, (h // 2 + 1) * 2 * LANES)

        def logits(c):
            sc = s_scr[h, :, c * LANES:(c + 1) * LANES]
            if c >= diag_from:
                sc = jnp.where(row >= lane + (c - diag_from) * LANES, sc, NEG_INF)
            return sc

        m_old = m_scr[h]
        m_blk = functools.reduce(jnp.maximum, [logits(c) for c in range(n_chunk)])
        m_new = jnp.maximum(m_old, jnp.max(m_blk, axis=-1, keepdims=True))
        alpha = jnp.exp2(m_old - m_new)
        p_bf = jnp.concatenate([jnp.exp2(logits(c) - m_new) for c in range(n_chunk)], axis=1).astype(BF16)
        m_scr[h] = m_new
        acc_scr[h] = jnp.concatenate([alpha, alpha], axis=1) * acc_scr[h]
        acc_scr[h] += jnp.dot(p_bf, v_ref[0, pl.ds(k0, n_chunk * LANES), vcols],
                              preferred_element_type=F32)

    all_chunks = KEY_TILE // LANES
    half_chunks = ATT_TILE // LANES
    last = N_ATTN_HEADS - 1

    def full_window(t, _):
        for h in range(N_ATTN_HEADS):
            if h < last:
                scores(t, h + 1, all_chunks)
            else:
                scores(t + 1, 0, all_chunks)
            update(t, h, all_chunks, all_chunks)
        return 0

    scores(0, 0, all_chunks)
    lax.fori_loop(0, n_full, full_window, 0)

    @pl.when(i % 2 == 0)
    def _():
        for h in range(N_ATTN_HEADS):
            if h < last:
                scores(n_full, h + 1, half_chunks)
            update(n_full, h, half_chunks, 0)

    @pl.when(i % 2 == 1)
    def _():
        for h in range(N_ATTN_HEADS):
            if h < last:
                scores(n_full, h + 1, all_chunks)
            update(n_full, h, all_chunks, half_chunks)

    outs = []
    for p in range(N_PAIRS):
        o0 = acc_scr[2 * p, :, :LANES] / acc_scr[2 * p, :, LANES:]
        o1 = acc_scr[2 * p + 1, :, :LANES] / acc_scr[2 * p + 1, :, LANES:]
        outs.append(jnp.where(first, o0, o1))
    o_ref[0] = _rms(jnp.concatenate(outs, axis=1), g_ref[...]).astype(BF16)


def _fox_attention(q, k, v, ccol, crow, g):
    batch, seq, _ = q.shape
    n_win = seq // KEY_TILE
    stat = pltpu.VMEM((N_ATTN_HEADS, ATT_TILE, LANES), F32)
    return pl.pallas_call(
        _fox_kernel,
        out_shape=jax.ShapeDtypeStruct((batch, seq, D_ATTN), BF16),
        grid=(batch, seq // ATT_TILE),
        in_specs=[pl.BlockSpec((1, ATT_TILE, D_ATTN), lambda b, i: (b, i, 0)),
                  pl.BlockSpec((1, seq, D_ATTN), lambda b, i: (b, 0, 0)),
                  pl.BlockSpec((1, seq, 2 * D_ATTN), lambda b, i: (b, 0, 0)),
                  pl.BlockSpec((1, ATT_TILE, N_ATTN_HEADS), lambda b, i: (b, i, 0)),
                  pl.BlockSpec((1, n_win, N_ATTN_HEADS, KEY_TILE), lambda b, i: (b, 0, 0, 0)),
                  _const_spec((1, D_ATTN))],
        out_specs=pl.BlockSpec((1, ATT_TILE, D_ATTN), lambda b, i: (b, i, 0)),
        scratch_shapes=[pltpu.VMEM((N_ATTN_HEADS, ATT_TILE, LANES), BF16), stat, stat,
                        pltpu.VMEM((N_ATTN_HEADS, ATT_TILE, 2 * LANES), F32),
                        pltpu.VMEM((N_ATTN_HEADS, ATT_TILE, KEY_TILE), F32)],
        compiler_params=pltpu.CompilerParams(
            dimension_semantics=("arbitrary", "arbitrary"), vmem_limit_bytes=VMEM_LIMIT),
        name="fox_attn",
    )(q, k, v, ccol, crow, g)


def _mix_out_kernel(x_ref, a_ref, u_ref, wout_ref, xg_ref, wq_ref, mk_ref, mv_ref, wo_ref,
                    o_ref, o_scr):
    y = jnp.dot(a_ref[0], wout_ref[0, :D_ATTN, :].astype(BF16), preferred_element_type=F32)
    y = y + jnp.dot(u_ref[0], wout_ref[0, D_ATTN:, :].astype(BF16), preferred_element_type=F32)
    x = x_ref[0] + y

    hq = _rms(x, xg_ref[...]).astype(BF16)
    q = jnp.dot(hq, wq_ref[0].astype(BF16), preferred_element_type=F32) * (1.0 / math.sqrt(XATTN_HEAD_DIM))
    q = q.astype(BF16)
    for h in range(N_XATTN_HEADS):
        cols = slice(h * XATTN_HEAD_DIM, (h + 1) * XATTN_HEAD_DIM)
        s = lax.dot_general(q[:, cols], mk_ref[0, 0, :, cols], (((1,), (1,)), ((), ())),
                            preferred_element_type=F32)
        p = jnp.exp(s - jnp.max(s, axis=-1, keepdims=True))
        p = p / jnp.sum(p, axis=-1, keepdims=True)
        o_scr[:, cols] = jnp.dot(p.astype(BF16), mv_ref[0, 0, :, cols],
                                 preferred_element_type=F32).astype(BF16)
    o_ref[0] = x + jnp.dot(o_scr[...], wo_ref[0].astype(BF16), preferred_element_type=F32)


def _mix_out(x, attn, u, wout, xg, wq, mk, mv, wo, layer):
    batch, seq, _ = x.shape
    tok = lambda w: pl.BlockSpec((1, OUT_TILE, w), lambda b, s: (b, s, 0))
    mem_spec = pl.BlockSpec((1, 1, N_MEM, D_MODEL), lambda b, s: (layer, b, 0, 0))
    return pl.pallas_call(
        _mix_out_kernel,
        out_shape=jax.ShapeDtypeStruct((batch, seq, D_MODEL), F32),
        grid=(batch, seq // OUT_TILE),
        in_specs=[tok(D_MODEL), tok(D_ATTN), tok(D_CONV),
                  _layer_spec((D_MODEL, D_MODEL), layer), _const_spec((1, D_MODEL)),
                  _layer_spec((D_MODEL, D_MODEL), layer), mem_spec, mem_spec,
                  _layer_spec((D_MODEL, D_MODEL), layer)],
        out_specs=tok(D_MODEL),
        scratch_shapes=[pltpu.VMEM((OUT_TILE, D_MODEL), BF16)],
        compiler_params=pltpu.CompilerParams(
            dimension_semantics=("arbitrary", "arbitrary"), vmem_limit_bytes=VMEM_LIMIT),
        name="mix_out",
    )(x, attn, u, wout, xg, wq, mk, mv, wo)


def kernel(x, mem, ffn1_norm_g, ffn1_w_gate, ffn1_w_up, ffn1_w_down, mix_norm_g, w_in, b_f, conv_w, conv_b, conv_ln_g, conv_ln_b, attn_out_g, conv_out_g, w_out, xattn_norm_g, mem_norm_g, xattn_w_q, xattn_w_kv, xattn_w_o, ffn2_norm_g, ffn2_w_gate, ffn2_w_up, ffn2_w_down, final_norm_g):
    batch, seq, _ = x.shape
    depth = w_in.shape[0]
    n_win = seq // KEY_TILE
    row = lambda v: v.reshape(1, -1)

    mem_k, mem_v = _memkv(mem, mem_norm_g.reshape(depth, 1, D_MODEL), xattn_w_kv)
    fg = row(final_norm_g)
    tri = jnp.triu(jnp.ones((TOK_TILE, TOK_TILE), BF16))
    w_in_t = jnp.swapaxes(w_in, 1, 2)

    for l in range(depth):
        x2d = _ffn(x.reshape(batch * seq, D_MODEL), row(ffn1_norm_g[l]), ffn1_w_gate, ffn1_w_up,
                   ffn1_w_down, fg, l, final=False)
        x = x2d.reshape(batch, seq, D_MODEL)

        cw = jnp.broadcast_to(conv_w[l][:, None, :], (CONV_WIDTH, SUBLANES, D_CONV))
        q, k, v, c, u = _mix_in(
            x, row(mix_norm_g[l]), w_in_t, b_f[l].reshape(N_ATTN_HEADS, 1), tri, cw, row(conv_b[l]),
            row(conv_ln_g[l]), row(conv_ln_b[l]), row(conv_out_g[l]), l)

        ccol = c.transpose(0, 2, 1)
        crow = c.reshape(batch, N_ATTN_HEADS, n_win, KEY_TILE).transpose(0, 2, 1, 3)
        attn = _fox_attention(q, k, v, ccol, crow, row(attn_out_g[l]))

        x = _mix_out(x, attn, u, w_out, row(xattn_norm_g[l]), xattn_w_q, mem_k, mem_v, xattn_w_o, l)

        last = l == depth - 1
        x2d = _ffn(x.reshape(batch * seq, D_MODEL), row(ffn2_norm_g[l]), ffn2_w_gate, ffn2_w_up,
                   ffn2_w_down, fg, l, final=last)
        x = x2d.reshape(batch, seq, D_MODEL)
    return x
```

```python
import functools
import math

import jax
import jax.numpy as jnp
from jax import lax
from jax.experimental import pallas as pl
from jax.experimental.pallas import tpu as pltpu

F32 = jnp.float32
BF16 = jnp.bfloat16

D_MODEL = 1024
N_MEM = 256
D_ATTN = 512
D_CONV = 512
HEAD_DIM = 64
N_ATTN_HEADS = 8
CONV_WIDTH = 31
N_XATTN_HEADS = 4
XATTN_HEAD_DIM = 256
D_FF = 2816
D_IN = 3 * D_ATTN + N_ATTN_HEADS + 2 * D_CONV
EPS = 1e-6
NEG_INF = -1e30
LOG2E = math.log2(math.e)

LANES = 128
SUBLANES = 8
MXU_DIM = 256
VMEM_LIMIT = 56 * 1024 * 1024

TOK_TILE = 512
ATT_TILE = 512
KEY_TILE = ATT_TILE
DIAG_BAND = 256
FF_CHUNK = MXU_DIM
CONV_HALO = 32
CONV_ROWS = 64
OUT_TILE = 1024
MEM_BATCH = 4
N_PAIRS = D_ATTN // LANES


def _rms(x, g):
    return x * lax.rsqrt(jnp.mean(x * x, axis=-1, keepdims=True) + EPS) * g


def _const_spec(shape):
    nd = len(shape)
    return pl.BlockSpec(shape, lambda *_: (0,) * nd, pipeline_mode=pl.Buffered(1))


def _layer_spec(shape, layer):
    nd = len(shape)
    return pl.BlockSpec((1,) + shape, lambda *_: (layer,) + (0,) * nd, pipeline_mode=pl.Buffered(1))


def _ffn_kernel(x_ref, g_ref, wg_ref, wu_ref, wd_ref, fg_ref, o_ref, h_scr, act_scr, *, final):
    x = x_ref[...]
    h_scr[...] = _rms(x, g_ref[...]).astype(BF16)
    for c in range(D_FF // FF_CHUNK):
        cols = slice(c * FF_CHUNK, (c + 1) * FF_CHUNK)
        h = h_scr[...]
        gate = jnp.dot(h, wg_ref[0, :, cols].astype(BF16), preferred_element_type=F32)
        up = jnp.dot(h, wu_ref[0, :, cols].astype(BF16), preferred_element_type=F32)
        act_scr[:, cols] = (gate * jax.nn.sigmoid(gate) * up).astype(BF16)
    y = jnp.dot(act_scr[...], wd_ref[0].astype(BF16), preferred_element_type=F32)
    out = x + 0.5 * y
    if final:
        out = _rms(out, fg_ref[...])
    o_ref[...] = out


def _ffn(x2d, g, wg, wu, wd, fg, layer, *, final):
    t = x2d.shape[0]
    tile = pl.BlockSpec((TOK_TILE, D_MODEL), lambda i: (i, 0))
    return pl.pallas_call(
        functools.partial(_ffn_kernel, final=final),
        out_shape=jax.ShapeDtypeStruct((t, D_MODEL), F32),
        grid=(t // TOK_TILE,),
        in_specs=[tile, _const_spec((1, D_MODEL)), _layer_spec((D_MODEL, D_FF), layer),
                  _layer_spec((D_MODEL, D_FF), layer), _layer_spec((D_FF, D_MODEL), layer),
                  _const_spec((1, D_MODEL))],
        out_specs=tile,
        scratch_shapes=[pltpu.VMEM((TOK_TILE, D_MODEL), BF16), pltpu.VMEM((TOK_TILE, D_FF), BF16)],
        compiler_params=pltpu.CompilerParams(
            dimension_semantics=("arbitrary",), vmem_limit_bytes=VMEM_LIMIT),
        name="ffn_final" if final else "ffn",
    )(x2d, g, wg, wu, wd, fg)


def _memkv_kernel(mem_ref, g_ref, w_ref, k_ref, v_ref):
    mem = mem_ref[...].reshape(MEM_BATCH * N_MEM, D_MODEL)
    mn = _rms(mem, g_ref[0]).astype(BF16)
    kv = jnp.dot(mn, w_ref[0].astype(BF16), preferred_element_type=F32)
    k_ref[0] = kv[:, :D_MODEL].astype(BF16).reshape(MEM_BATCH, N_MEM, D_MODEL)
    v_ref[0] = kv[:, D_MODEL:].astype(BF16).reshape(MEM_BATCH, N_MEM, D_MODEL)


def _memkv(mem, g, w_kv):
    depth, batch = g.shape[0], mem.shape[0]
    out = jax.ShapeDtypeStruct((depth, batch, N_MEM, D_MODEL), BF16)
    out_spec = pl.BlockSpec((1, MEM_BATCH, N_MEM, D_MODEL), lambda l, b: (l, b, 0, 0))
    return pl.pallas_call(
        _memkv_kernel,
        out_shape=(out, out),
        grid=(depth, batch // MEM_BATCH),
        in_specs=[pl.BlockSpec((MEM_BATCH, N_MEM, D_MODEL), lambda l, b: (b, 0, 0)),
                  pl.BlockSpec((1, 1, D_MODEL), lambda l, b: (l, 0, 0)),
                  pl.BlockSpec((1, D_MODEL, 2 * D_MODEL), lambda l, b: (l, 0, 0))],
        out_specs=(out_spec, out_spec),
        compiler_params=pltpu.CompilerParams(
            dimension_semantics=("arbitrary", "arbitrary"), vmem_limit_bytes=VMEM_LIMIT),
        name="mem_kv",
    )(mem, g, w_kv)


def _log_sigmoid(z):
    return jnp.minimum(z, 0.0) - jnp.log1p(jnp.exp(-jnp.abs(z)))


def _split3(a):
    hi = a.astype(BF16).astype(F32)
    r = a - hi
    mid = r.astype(BF16).astype(F32)
    lo = (r - mid).astype(BF16).astype(F32)
    return hi, mid, lo


def _mix_in_kernel(x_ref, g_ref, w_ref, bf_ref, tri_ref, cw_ref, cb_ref, lng_ref,
                   lnb_ref, cog_ref, q_ref, k_ref, v_ref, c_ref, u_ref,
                   wqkv_scr, wf_scr, wag_scr, h_scr, ubuf, shifted, carry):
    s_idx = pl.program_id(1)

    @pl.when(jnp.logical_and(pl.program_id(0) == 0, s_idx == 0))
    def _():
        f0 = 3 * D_ATTN
        wqkv_scr[...] = jnp.transpose(w_ref[0, :f0, :]).astype(BF16)
        wf_scr[...] = jnp.transpose(w_ref[0, f0:f0 + LANES, :]).astype(BF16)
        wag_scr[...] = jnp.transpose(w_ref[0, f0 + N_ATTN_HEADS:, :]).astype(BF16)

    def project(w_scr):
        return jnp.dot(h_scr[...], w_scr[...], preferred_element_type=F32)

    @pl.when(s_idx == 0)
    def _():
        carry[...] = jnp.zeros_like(carry)
        ubuf[0:CONV_HALO, :] = jnp.zeros((CONV_HALO, D_CONV), F32)

    h_scr[...] = _rms(x_ref[0], g_ref[...]).astype(BF16)

    ag = project(wag_scr)
    ubuf[CONV_HALO:, :] = ag[:, :D_CONV] * jax.nn.sigmoid(ag[:, D_CONV:])
    n_shift = TOK_TILE + CONV_HALO - SUBLANES
    for r in range(1, SUBLANES):
        shifted[r - 1] = ubuf[r:r + n_shift, :]

    qkv = project(wqkv_scr)
    q_ref[0] = (qkv[:, :D_ATTN] * (LOG2E / math.sqrt(HEAD_DIM))).astype(BF16)
    k_ref[0] = qkv[:, D_ATTN:2 * D_ATTN].astype(BF16)
    ones = jnp.ones((TOK_TILE, LANES), BF16)
    for p in range(N_PAIRS):
        v_ref[0, :, 2 * p * LANES:(2 * p + 1) * LANES] = (
            qkv[:, 2 * D_ATTN + p * LANES:2 * D_ATTN + (p + 1) * LANES].astype(BF16))
        v_ref[0, :, (2 * p + 1) * LANES:(2 * p + 2) * LANES] = ones

    f_t = jnp.transpose(project(wf_scr))
    logf = _log_sigmoid(f_t[:N_ATTN_HEADS, :] + bf_ref[...]) * LOG2E
    parts = jnp.concatenate(_split3(logf) + (jnp.zeros_like(logf),), axis=0).astype(BF16)
    sums = jnp.dot(parts, tri_ref[...], preferred_element_type=F32)
    h8 = N_ATTN_HEADS
    c_tile = carry[...] + sums[:h8] + sums[h8:2 * h8] + sums[2 * h8:3 * h8]
    c_ref[0] = c_tile
    carry[...] = c_tile[:, TOK_TILE - 1:TOK_TILE]

    for rb in range(TOK_TILE // CONV_ROWS):
        r0 = rb * CONV_ROWS
        acc = jnp.zeros((CONV_ROWS // SUBLANES, SUBLANES, D_CONV), F32)
        for tap in range(CONV_WIDTH):
            off = r0 + tap + CONV_HALO - (CONV_WIDTH - 1)
            sh, al = off % SUBLANES, (off // SUBLANES) * SUBLANES
            if sh == 0:
                blk = ubuf[al:al + CONV_ROWS, :]
            else:
                blk = shifted[sh - 1, al:al + CONV_ROWS, :]
            acc = acc + blk.reshape(CONV_ROWS // SUBLANES, SUBLANES, D_CONV) * cw_ref[tap]
        y = acc.reshape(CONV_ROWS, D_CONV) + cb_ref[...]
        mu = jnp.mean(y, axis=-1, keepdims=True)
        yc = y - mu
        yn = yc * lax.rsqrt(jnp.mean(yc * yc, axis=-1, keepdims=True) + EPS) * lng_ref[...] + lnb_ref[...]
        sw = yn * jax.nn.sigmoid(yn)
        u_ref[0, r0:r0 + CONV_ROWS, :] = _rms(sw, cog_ref[...]).astype(BF16)

    ubuf[0:CONV_HALO, :] = ubuf[TOK_TILE:TOK_TILE + CONV_HALO, :]


def _mix_in(x, g, w, bf, tri, cw, cb, lng, lnb, cog, layer):
    batch, seq, _ = x.shape
    tok = lambda w: pl.BlockSpec((1, TOK_TILE, w), lambda b, s: (b, s, 0))
    n_shift = TOK_TILE + CONV_HALO - SUBLANES
    outs = (jax.ShapeDtypeStruct((batch, seq, D_ATTN), BF16),) * 2 + (
        jax.ShapeDtypeStruct((batch, seq, 2 * D_ATTN), BF16),
        jax.ShapeDtypeStruct((batch, N_ATTN_HEADS, seq), F32),
        jax.ShapeDtypeStruct((batch, seq, D_CONV), BF16))
    return pl.pallas_call(
        _mix_in_kernel,
        out_shape=outs,
        grid=(batch, seq // TOK_TILE),
        in_specs=[tok(D_MODEL), _const_spec((1, D_MODEL)), _layer_spec((D_IN, D_MODEL), layer),
                  _const_spec((N_ATTN_HEADS, 1)), _const_spec((TOK_TILE, TOK_TILE)),
                  _const_spec((CONV_WIDTH, SUBLANES, D_CONV)),
                  _const_spec((1, D_CONV)), _const_spec((1, D_CONV)), _const_spec((1, D_CONV)),
                  _const_spec((1, D_CONV))],
        out_specs=(tok(D_ATTN), tok(D_ATTN), tok(2 * D_ATTN),
                   pl.BlockSpec((1, N_ATTN_HEADS, TOK_TILE), lambda b, s: (b, 0, s)), tok(D_CONV)),
        scratch_shapes=[pltpu.VMEM((D_MODEL, 3 * D_ATTN), BF16), pltpu.VMEM((D_MODEL, LANES), BF16),
                        pltpu.VMEM((D_MODEL, 2 * D_CONV), BF16),
                        pltpu.VMEM((TOK_TILE, D_MODEL), BF16),
                        pltpu.VMEM((TOK_TILE + CONV_HALO, D_CONV), F32),
                        pltpu.VMEM((SUBLANES - 1, n_shift, D_CONV), F32),
                        pltpu.VMEM((N_ATTN_HEADS, 1), F32)],
        compiler_params=pltpu.CompilerParams(
            dimension_semantics=("arbitrary", "arbitrary"), vmem_limit_bytes=VMEM_LIMIT),
        name="mix_in",
    )(x, g, w, bf, tri, cw, cb, lng, lnb, cog)


def _fox_kernel(q_ref, k_ref, v_ref, ccol_ref, crow_ref, g_ref, o_ref,
                qm_scr, ct_scr, m_scr, acc_scr, s_scr):
    i = pl.program_id(1)
    n_full = i
    lane = lax.broadcasted_iota(jnp.int32, (ATT_TILE, LANES), 1)
    first = lane < HEAD_DIM

    c_t = ccol_ref[0]
    for p in range(N_PAIRS):
        qp = q_ref[0, :, p * LANES:(p + 1) * LANES]
        zero = jnp.zeros_like(qp)
        qm_scr[2 * p] = jnp.where(first, qp, zero)
        qm_scr[2 * p + 1] = jnp.where(first, zero, qp)
    for h in range(N_ATTN_HEADS):
        ct_scr[h] = jnp.broadcast_to(c_t[:, h:h + 1], (ATT_TILE, LANES))
    m_scr[...] = jnp.full(m_scr.shape, NEG_INF, F32)
    acc_scr[...] = jnp.zeros(acc_scr.shape, F32)

    def scores(t, h, n_chunk):
        k0 = pl.multiple_of(t * KEY_TILE, KEY_TILE)
        pair = slice((h // 2) * LANES, (h // 2 + 1) * LANES)
        c_s = crow_ref[0, t]
        s = lax.dot_general(qm_scr[h], k_ref[0, pl.ds(k0, n_chunk * LANES), pair],
                            (((1,), (1,)), ((), ())), preferred_element_type=F32)
        ct = ct_scr[h]
        for c in range(n_chunk):
            cols = slice(c * LANES, (c + 1) * LANES)
            s_scr[h, :, cols] = s[:, cols] + ct - c_s[h:h + 1, cols]

    def update(t, h, r0, n_rows, n_chunk, diagonal):
        k0 = pl.multiple_of(t * KEY_TILE, KEY_TILE)
        rows = slice(r0, r0 + n_rows)
        vcols = slice((h // 2) * 2 * LANES, (h // 2 + 1) * 2 * LANES)

        q_pos = lax.broadcasted_iota(jnp.int32, (n_rows, LANES), 0) + r0
        k_pos = lax.broadcasted_iota(jnp.int32, (n_rows, LANES), 1)

        def logits(c):
            sc = s_scr[h, rows, c * LANES:(c + 1) * LANES]
            if diagonal and (c + 1) * LANES > r0:
                sc = jnp.where(q_pos >= k_pos + c * LANES, sc, NEG_INF)
            return sc

        m_old = m_scr[h, rows, :]
        m_blk = functools.reduce(jnp.maximum, [logits(c) for c in range(n_chunk)])
        m_new = jnp.maximum(m_old, jnp.max(m_blk, axis=-1, keepdims=True))
        alpha = jnp.exp2(m_old - m_new)
        p_bf = jnp.concatenate([jnp.exp2(logits(c) - m_new) for c in range(n_chunk)], axis=1).astype(BF16)
        m_scr[h, rows, :] = m_new
        acc_scr[h, rows, :] = jnp.concatenate([alpha, alpha], axis=1) * acc_scr[h, rows, :]
        acc_scr[h, rows, :] += jnp.dot(p_bf, v_ref[0, pl.ds(k0, n_chunk * LANES), vcols],
                                       preferred_element_type=F32)

    all_chunks = KEY_TILE // LANES
    last = N_ATTN_HEADS - 1

    def full_window(t, _):
        for h in range(N_ATTN_HEADS):
            if h < last:
                scores(t, h + 1, all_chunks)
            else:
                scores(t + 1, 0, all_chunks)
            update(t, h, 0, ATT_TILE, all_chunks, False)
        return 0

    scores(0, 0, all_chunks)
    lax.fori_loop(0, n_full, full_window, 0)

    for h in range(N_ATTN_HEADS):
        if h < last:
            scores(n_full, h + 1, all_chunks)
        for r0 in range(0, ATT_TILE, DIAG_BAND):
            update(n_full, h, r0, DIAG_BAND, (r0 + DIAG_BAND) // LANES, True)

    outs = []
    for p in range(N_PAIRS):
        o0 = acc_scr[2 * p, :, :LANES] / acc_scr[2 * p, :, LANES:]
        o1 = acc_scr[2 * p + 1, :, :LANES] / acc_scr[2 * p + 1, :, LANES:]
        outs.append(jnp.where(first, o0, o1))
    o_ref[0] = _rms(jnp.concatenate(outs, axis=1), g_ref[...]).astype(BF16)


def _fox_attention(q, k, v, ccol, crow, g):
    batch, seq, _ = q.shape
    n_win = seq // KEY_TILE
    stat = pltpu.VMEM((N_ATTN_HEADS, ATT_TILE, LANES), F32)
    return pl.pallas_call(
        _fox_kernel,
        out_shape=jax.ShapeDtypeStruct((batch, seq, D_ATTN), BF16),
        grid=(batch, seq // ATT_TILE),
        in_specs=[pl.BlockSpec((1, ATT_TILE, D_ATTN), lambda b, i: (b, i, 0)),
                  pl.BlockSpec((1, seq, D_ATTN), lambda b, i: (b, 0, 0)),
                  pl.BlockSpec((1, seq, 2 * D_ATTN), lambda b, i: (b, 0, 0)),
                  pl.BlockSpec((1, ATT_TILE, N_ATTN_HEADS), lambda b, i: (b, i, 0)),
                  pl.BlockSpec((1, n_win, N_ATTN_HEADS, KEY_TILE), lambda b, i: (b, 0, 0, 0)),
                  _const_spec((1, D_ATTN))],
        out_specs=pl.BlockSpec((1, ATT_TILE, D_ATTN), lambda b, i: (b, i, 0)),
        scratch_shapes=[pltpu.VMEM((N_ATTN_HEADS, ATT_TILE, LANES), BF16), stat, stat,
                        pltpu.VMEM((N_ATTN_HEADS, ATT_TILE, 2 * LANES), F32),
                        pltpu.VMEM((N_ATTN_HEADS, ATT_TILE, KEY_TILE), F32)],
        compiler_params=pltpu.CompilerParams(
            dimension_semantics=("arbitrary", "arbitrary"), vmem_limit_bytes=VMEM_LIMIT),
        name="fox_attn",
    )(q, k, v, ccol, crow, g)


def _mix_out_kernel(x_ref, a_ref, u_ref, wout_ref, xg_ref, wq_ref, mk_ref, mv_ref, wo_ref,
                    o_ref, o_scr):
    y = jnp.dot(a_ref[0], wout_ref[0, :D_ATTN, :].astype(BF16), preferred_element_type=F32)
    y = y + jnp.dot(u_ref[0], wout_ref[0, D_ATTN:, :].astype(BF16), preferred_element_type=F32)
    x = x_ref[0] + y

    hq = _rms(x, xg_ref[...]).astype(BF16)
    q = jnp.dot(hq, wq_ref[0].astype(BF16), preferred_element_type=F32) * (1.0 / math.sqrt(XATTN_HEAD_DIM))
    q = q.astype(BF16)
    for h in range(N_XATTN_HEADS):
        cols = slice(h * XATTN_HEAD_DIM, (h + 1) * XATTN_HEAD_DIM)
        s = lax.dot_general(q[:, cols], mk_ref[0, 0, :, cols], (((1,), (1,)), ((), ())),
                            preferred_element_type=F32)
        p = jnp.exp(s - jnp.max(s, axis=-1, keepdims=True))
        p = p / jnp.sum(p, axis=-1, keepdims=True)
        o_scr[:, cols] = jnp.dot(p.astype(BF16), mv_ref[0, 0, :, cols],
                                 preferred_element_type=F32).astype(BF16)
    o_ref[0] = x + jnp.dot(o_scr[...], wo_ref[0].astype(BF16), preferred_element_type=F32)


def _mix_out(x, attn, u, wout, xg, wq, mk, mv, wo, layer):
    batch, seq, _ = x.shape
    tok = lambda w: pl.BlockSpec((1, OUT_TILE, w), lambda b, s: (b, s, 0))
    mem_spec = pl.BlockSpec((1, 1, N_MEM, D_MODEL), lambda b, s: (layer, b, 0, 0))
    return pl.pallas_call(
        _mix_out_kernel,
        out_shape=jax.ShapeDtypeStruct((batch, seq, D_MODEL), F32),
        grid=(batch, seq // OUT_TILE),
        in_specs=[tok(D_MODEL), tok(D_ATTN), tok(D_CONV),
                  _layer_spec((D_MODEL, D_MODEL), layer), _const_spec((1, D_MODEL)),
                  _layer_spec((D_MODEL, D_MODEL), layer), mem_spec, mem_spec,
                  _layer_spec((D_MODEL, D_MODEL), layer)],
        out_specs=tok(D_MODEL),
        scratch_shapes=[pltpu.VMEM((OUT_TILE, D_MODEL), BF16)],
        compiler_params=pltpu.CompilerParams(
            dimension_semantics=("arbitrary", "arbitrary"), vmem_limit_bytes=VMEM_LIMIT),
        name="mix_out",
    )(x, attn, u, wout, xg, wq, mk, mv, wo)


def kernel(x, mem, ffn1_norm_g, ffn1_w_gate, ffn1_w_up, ffn1_w_down, mix_norm_g, w_in, b_f, conv_w, conv_b, conv_ln_g, conv_ln_b, attn_out_g, conv_out_g, w_out, xattn_norm_g, mem_norm_g, xattn_w_q, xattn_w_kv, xattn_w_o, ffn2_norm_g, ffn2_w_gate, ffn2_w_up, ffn2_w_down, final_norm_g):
    batch, seq, _ = x.shape
    depth = w_in.shape[0]
    n_win = seq // KEY_TILE
    row = lambda v: v.reshape(1, -1)

    mem_k, mem_v = _memkv(mem, mem_norm_g.reshape(depth, 1, D_MODEL), xattn_w_kv)
    fg = row(final_norm_g)
    tri = jnp.triu(jnp.ones((TOK_TILE, TOK_TILE), BF16))
    w_in_t = jnp.swapaxes(w_in, 1, 2)

    for l in range(depth):
        x2d = _ffn(x.reshape(batch * seq, D_MODEL), row(ffn1_norm_g[l]), ffn1_w_gate, ffn1_w_up,
                   ffn1_w_down, fg, l, final=False)
        x = x2d.reshape(batch, seq, D_MODEL)

        cw = jnp.broadcast_to(conv_w[l][:, None, :], (CONV_WIDTH, SUBLANES, D_CONV))
        q, k, v, c, u = _mix_in(
            x, row(mix_norm_g[l]), w_in_t, b_f[l].reshape(N_ATTN_HEADS, 1), tri, cw, row(conv_b[l]),
            row(conv_ln_g[l]), row(conv_ln_b[l]), row(conv_out_g[l]), l)

        ccol = c.transpose(0, 2, 1)
        crow = c.reshape(batch, N_ATTN_HEADS, n_win, KEY_TILE).transpose(0, 2, 1, 3)
        attn = _fox_attention(q, k, v, ccol, crow, row(attn_out_g[l]))

        x = _mix_out(x, attn, u, w_out, row(xattn_norm_g[l]), xattn_w_q, mem_k, mem_v, xattn_w_o, l)

        last = l == depth - 1
        x2d = _ffn(x.reshape(batch * seq, D_MODEL), row(ffn2_norm_g[l]), ffn2_w_gate, ffn2_w_up,
                   ffn2_w_down, fg, l, final=last)
        x = x2d.reshape(batch, seq, D_MODEL)
    return x
```

```python
import functools
import math

import jax
import jax.numpy as jnp
from jax import lax
from jax.experimental import pallas as pl
from jax.experimental.pallas import tpu as pltpu

F32 = jnp.float32
BF16 = jnp.bfloat16

D_MODEL = 1024
N_MEM = 256
D_ATTN = 512
D_CONV = 512
HEAD_DIM = 64
N_ATTN_HEADS = 8
CONV_WIDTH = 31
N_XATTN_HEADS = 4
XATTN_HEAD_DIM = 256
D_FF = 2816
D_IN = 3 * D_ATTN + N_ATTN_HEADS + 2 * D_CONV
EPS = 1e-6
NEG_INF = -1e30
LOG2E = math.log2(math.e)

LANES = 128
SUBLANES = 8
MXU_DIM = 256
VMEM_LIMIT = 56 * 1024 * 1024

TOK_TILE = 512
ATT_TILE = 512
KEY_TILE = ATT_TILE
DIAG_BAND = 256
FF_CHUNK = MXU_DIM
CONV_HALO = 32
CONV_ROWS = 64
OUT_TILE = 1024
MEM_BATCH = 4
N_PAIRS = D_ATTN // LANES


def _rms(x, g):
    return x * lax.rsqrt(jnp.mean(x * x, axis=-1, keepdims=True) + EPS) * g


def _const_spec(shape):
    nd = len(shape)
    return pl.BlockSpec(shape, lambda *_: (0,) * nd, pipeline_mode=pl.Buffered(1))


def _layer_spec(shape, layer):
    nd = len(shape)
    return pl.BlockSpec((1,) + shape, lambda *_: (layer,) + (0,) * nd, pipeline_mode=pl.Buffered(1))


def _ffn_kernel(x_ref, xn_ref, g_ref, wg_ref, wu_ref, wd_ref, fg_ref, o_ref,
                h_scr, hn_scr, act_scr, act0_scr, *, final):
    def hidden_chunk(h_ref, c):
        cols = slice(c * FF_CHUNK, (c + 1) * FF_CHUNK)
        h = h_ref[...]
        gate = jnp.dot(h, wg_ref[0, :, cols].astype(BF16), preferred_element_type=F32)
        up = jnp.dot(h, wu_ref[0, :, cols].astype(BF16), preferred_element_type=F32)
        return (gate * jax.nn.sigmoid(gate) * up).astype(BF16)

    @pl.when(pl.program_id(0) == 0)
    def _():
        hn_scr[...] = _rms(x_ref[...], g_ref[...]).astype(BF16)
        act0_scr[...] = hidden_chunk(hn_scr, 0)

    h_scr[...] = hn_scr[...]
    act_scr[:, :FF_CHUNK] = act0_scr[...]
    for c in range(1, D_FF // FF_CHUNK):
        act_scr[:, c * FF_CHUNK:(c + 1) * FF_CHUNK] = hidden_chunk(h_scr, c)
    hn_scr[...] = _rms(xn_ref[...], g_ref[...]).astype(BF16)
    y = jnp.dot(act_scr[...], wd_ref[0].astype(BF16), preferred_element_type=F32)
    act0_scr[...] = hidden_chunk(hn_scr, 0)
    out = x_ref[...] + 0.5 * y
    if final:
        out = _rms(out, fg_ref[...])
    o_ref[...] = out


def _ffn(x2d, g, wg, wu, wd, fg, layer, *, final):
    t = x2d.shape[0]
    n_tiles = t // TOK_TILE
    tile = pl.BlockSpec((TOK_TILE, D_MODEL), lambda i: (i, 0))
    next_tile = pl.BlockSpec((TOK_TILE, D_MODEL), lambda i: (jnp.minimum(i + 1, n_tiles - 1), 0))
    return pl.pallas_call(
        functools.partial(_ffn_kernel, final=final),
        out_shape=jax.ShapeDtypeStruct((t, D_MODEL), F32),
        grid=(n_tiles,),
        in_specs=[tile, next_tile, _const_spec((1, D_MODEL)), _layer_spec((D_MODEL, D_FF), layer),
                  _layer_spec((D_MODEL, D_FF), layer), _layer_spec((D_FF, D_MODEL), layer),
                  _const_spec((1, D_MODEL))],
        out_specs=tile,
        scratch_shapes=[pltpu.VMEM((TOK_TILE, D_MODEL), BF16), pltpu.VMEM((TOK_TILE, D_MODEL), BF16),
                        pltpu.VMEM((TOK_TILE, D_FF), BF16), pltpu.VMEM((TOK_TILE, FF_CHUNK), BF16)],
        compiler_params=pltpu.CompilerParams(
            dimension_semantics=("arbitrary",), vmem_limit_bytes=VMEM_LIMIT),
        name="ffn_final" if final else "ffn",
    )(x2d, x2d, g, wg, wu, wd, fg)


def _memkv_kernel(mem_ref, g_ref, w_ref, k_ref, v_ref):
    mem = mem_ref[...].reshape(MEM_BATCH * N_MEM, D_MODEL)
    mn = _rms(mem, g_ref[0]).astype(BF16)
    kv = jnp.dot(mn, w_ref[0].astype(BF16), preferred_element_type=F32)
    k_ref[0] = kv[:, :D_MODEL].astype(BF16).reshape(MEM_BATCH, N_MEM, D_MODEL)
    v_ref[0] = kv[:, D_MODEL:].astype(BF16).reshape(MEM_BATCH, N_MEM, D_MODEL)


def _memkv(mem, g, w_kv):
    depth, batch = g.shape[0], mem.shape[0]
    out = jax.ShapeDtypeStruct((depth, batch, N_MEM, D_MODEL), BF16)
    out_spec = pl.BlockSpec((1, MEM_BATCH, N_MEM, D_MODEL), lambda l, b: (l, b, 0, 0))
    return pl.pallas_call(
        _memkv_kernel,
        out_shape=(out, out),
        grid=(depth, batch // MEM_BATCH),
        in_specs=[pl.BlockSpec((MEM_BATCH, N_MEM, D_MODEL), lambda l, b: (b, 0, 0)),
                  pl.BlockSpec((1, 1, D_MODEL), lambda l, b: (l, 0, 0)),
                  pl.BlockSpec((1, D_MODEL, 2 * D_MODEL), lambda l, b: (l, 0, 0))],
        out_specs=(out_spec, out_spec),
        compiler_params=pltpu.CompilerParams(
            dimension_semantics=("arbitrary", "arbitrary"), vmem_limit_bytes=VMEM_LIMIT),
        name="mem_kv",
    )(mem, g, w_kv)


def _log_sigmoid(z):
    return jnp.minimum(z, 0.0) - jnp.log1p(jnp.exp(-jnp.abs(z)))


def _split3(a):
    hi = a.astype(BF16).astype(F32)
    r = a - hi
    mid = r.astype(BF16).astype(F32)
    lo = (r - mid).astype(BF16).astype(F32)
    return hi, mid, lo


def _mix_in_kernel(x_ref, g_ref, w_ref, bf_ref, tri_ref, cw_ref, cb_ref, lng_ref,
                   lnb_ref, cog_ref, q_ref, k_ref, v_ref, c_ref, u_ref,
                   wqkv_scr, wf_scr, wag_scr, h_scr, ubuf, shifted, carry):
    s_idx = pl.program_id(1)

    @pl.when(jnp.logical_and(pl.program_id(0) == 0, s_idx == 0))
    def _():
        f0 = 3 * D_ATTN
        wqkv_scr[...] = jnp.transpose(w_ref[0, :f0, :]).astype(BF16)
        wf_scr[...] = jnp.transpose(w_ref[0, f0:f0 + LANES, :]).astype(BF16)
        wag_scr[...] = jnp.transpose(w_ref[0, f0 + N_ATTN_HEADS:, :]).astype(BF16)

    def project(w_scr):
        return jnp.dot(h_scr[...], w_scr[...], preferred_element_type=F32)

    @pl.when(s_idx == 0)
    def _():
        carry[...] = jnp.zeros_like(carry)
        ubuf[0:CONV_HALO, :] = jnp.zeros((CONV_HALO, D_CONV), F32)

    h_scr[...] = _rms(x_ref[0], g_ref[...]).astype(BF16)

    ag = project(wag_scr)
    ubuf[CONV_HALO:, :] = ag[:, :D_CONV] * jax.nn.sigmoid(ag[:, D_CONV:])
    n_shift = TOK_TILE + CONV_HALO - SUBLANES
    for r in range(1, SUBLANES):
        shifted[r - 1] = ubuf[r:r + n_shift, :]

    qkv = project(wqkv_scr)
    q_ref[0] = (qkv[:, :D_ATTN] * (LOG2E / math.sqrt(HEAD_DIM))).astype(BF16)
    k_ref[0] = qkv[:, D_ATTN:2 * D_ATTN].astype(BF16)
    ones = jnp.ones((TOK_TILE, LANES), BF16)
    for p in range(N_PAIRS):
        v_ref[0, :, 2 * p * LANES:(2 * p + 1) * LANES] = (
            qkv[:, 2 * D_ATTN + p * LANES:2 * D_ATTN + (p + 1) * LANES].astype(BF16))
        v_ref[0, :, (2 * p + 1) * LANES:(2 * p + 2) * LANES] = ones

    f_t = jnp.transpose(project(wf_scr))
    logf = _log_sigmoid(f_t[:N_ATTN_HEADS, :] + bf_ref[...]) * LOG2E
    parts = jnp.concatenate(_split3(logf) + (jnp.zeros_like(logf),), axis=0).astype(BF16)
    sums = jnp.dot(parts, tri_ref[...], preferred_element_type=F32)
    h8 = N_ATTN_HEADS
    c_tile = carry[...] + sums[:h8] + sums[h8:2 * h8] + sums[2 * h8:3 * h8]
    c_ref[0] = c_tile
    carry[...] = c_tile[:, TOK_TILE - 1:TOK_TILE]

    for rb in range(TOK_TILE // CONV_ROWS):
        r0 = rb * CONV_ROWS
        acc = jnp.zeros((CONV_ROWS // SUBLANES, SUBLANES, D_CONV), F32)
        for tap in range(CONV_WIDTH):
            off = r0 + tap + CONV_HALO - (CONV_WIDTH - 1)
            sh, al = off % SUBLANES, (off // SUBLANES) * SUBLANES
            if sh == 0:
                blk = ubuf[al:al + CONV_ROWS, :]
            else:
                blk = shifted[sh - 1, al:al + CONV_ROWS, :]
            acc = acc + blk.reshape(CONV_ROWS // SUBLANES, SUBLANES, D_CONV) * cw_ref[tap]
        y = acc.reshape(CONV_ROWS, D_CONV) + cb_ref[...]
        mu = jnp.mean(y, axis=-1, keepdims=True)
        yc = y - mu
        yn = yc * lax.rsqrt(jnp.mean(yc * yc, axis=-1, keepdims=True) + EPS) * lng_ref[...] + lnb_ref[...]
        sw = yn * jax.nn.sigmoid(yn)
        u_ref[0, r0:r0 + CONV_ROWS, :] = _rms(sw, cog_ref[...]).astype(BF16)

    ubuf[0:CONV_HALO, :] = ubuf[TOK_TILE:TOK_TILE + CONV_HALO, :]


def _mix_in(x, g, w, bf, tri, cw, cb, lng, lnb, cog, layer):
    batch, seq, _ = x.shape
    tok = lambda w: pl.BlockSpec((1, TOK_TILE, w), lambda b, s: (b, s, 0))
    n_shift = TOK_TILE + CONV_HALO - SUBLANES
    outs = (jax.ShapeDtypeStruct((batch, seq, D_ATTN), BF16),) * 2 + (
        jax.ShapeDtypeStruct((batch, seq, 2 * D_ATTN), BF16),
        jax.ShapeDtypeStruct((batch, N_ATTN_HEADS, seq), F32),
        jax.ShapeDtypeStruct((batch, seq, D_CONV), BF16))
    return pl.pallas_call(
        _mix_in_kernel,
        out_shape=outs,
        grid=(batch, seq // TOK_TILE),
        in_specs=[tok(D_MODEL), _const_spec((1, D_MODEL)), _layer_spec((D_IN, D_MODEL), layer),
                  _const_spec((N_ATTN_HEADS, 1)), _const_spec((TOK_TILE, TOK_TILE)),
                  _const_spec((CONV_WIDTH, SUBLANES, D_CONV)),
                  _const_spec((1, D_CONV)), _const_spec((1, D_CONV)), _const_spec((1, D_CONV)),
                  _const_spec((1, D_CONV))],
        out_specs=(tok(D_ATTN), tok(D_ATTN), tok(2 * D_ATTN),
                   pl.BlockSpec((1, N_ATTN_HEADS, TOK_TILE), lambda b, s: (b, 0, s)), tok(D_CONV)),
        scratch_shapes=[pltpu.VMEM((D_MODEL, 3 * D_ATTN), BF16), pltpu.VMEM((D_MODEL, LANES), BF16),
                        pltpu.VMEM((D_MODEL, 2 * D_CONV), BF16),
                        pltpu.VMEM((TOK_TILE, D_MODEL), BF16),
                        pltpu.VMEM((TOK_TILE + CONV_HALO, D_CONV), F32),
                        pltpu.VMEM((SUBLANES - 1, n_shift, D_CONV), F32),
                        pltpu.VMEM((N_ATTN_HEADS, 1), F32)],
        compiler_params=pltpu.CompilerParams(
            dimension_semantics=("arbitrary", "arbitrary"), vmem_limit_bytes=VMEM_LIMIT),
        name="mix_in",
    )(x, g, w, bf, tri, cw, cb, lng, lnb, cog)


def _fox_kernel(q_ref, k_ref, v_ref, ccol_ref, crow_ref, g_ref, o_ref,
                qm_scr, ct_scr, m_scr, acc_scr, s_scr):
    i = pl.program_id(1)
    n_full = i
    lane = lax.broadcasted_iota(jnp.int32, (ATT_TILE, LANES), 1)
    first = lane < HEAD_DIM

    c_t = ccol_ref[0]
    for p in range(N_PAIRS):
        qp = q_ref[0, :, p * LANES:(p + 1) * LANES]
        zero = jnp.zeros_like(qp)
        qm_scr[2 * p] = jnp.where(first, qp, zero)
        qm_scr[2 * p + 1] = jnp.where(first, zero, qp)
    for h in range(N_ATTN_HEADS):
        ct_scr[h] = jnp.broadcast_to(c_t[:, h:h + 1], (ATT_TILE, LANES))
    m_scr[...] = jnp.full(m_scr.shape, NEG_INF, F32)
    acc_scr[...] = jnp.zeros(acc_scr.shape, F32)

    def scores(t, h, n_chunk):
        k0 = pl.multiple_of(t * KEY_TILE, KEY_TILE)
        pair = slice((h // 2) * LANES, (h // 2 + 1) * LANES)
        c_s = crow_ref[0, t]
        s = lax.dot_general(qm_scr[h], k_ref[0, pl.ds(k0, n_chunk * LANES), pair],
                            (((1,), (1,)), ((), ())), preferred_element_type=F32)
        ct = ct_scr[h]
        for c in range(n_chunk):
            cols = slice(c * LANES, (c + 1) * LANES)
            s_scr[h, :, cols] = s[:, cols] + ct - c_s[h:h + 1, cols]

    def update(t, h, r0, n_rows, n_chunk, diagonal):
        k0 = pl.multiple_of(t * KEY_TILE, KEY_TILE)
        rows = slice(r0, r0 + n_rows)
        vcols = slice((h // 2) * 2 * LANES, (h // 2 + 1) * 2 * LANES)

        q_pos = lax.broadcasted_iota(jnp.int32, (n_rows, LANES), 0) + r0
        k_pos = lax.broadcasted_iota(jnp.int32, (n_rows, LANES), 1)

        def logits(c):
            sc = s_scr[h, rows, c * LANES:(c + 1) * LANES]
            if diagonal and (c + 1) * LANES > r0:
                sc = jnp.where(q_pos >= k_pos + c * LANES, sc, NEG_INF)
            return sc

        m_old = m_scr[h, rows, :]
        m_blk = functools.reduce(jnp.maximum, [logits(c) for c in range(n_chunk)])
        m_new = jnp.maximum(m_old, jnp.max(m_blk, axis=-1, keepdims=True))
        alpha = jnp.exp2(m_old - m_new)
        p_bf = jnp.concatenate([jnp.exp2(logits(c) - m_new) for c in range(n_chunk)], axis=1).astype(BF16)
        m_scr[h, rows, :] = m_new
        acc_scr[h, rows, :] = jnp.concatenate([alpha, alpha], axis=1) * acc_scr[h, rows, :]
        acc_scr[h, rows, :] += jnp.dot(p_bf, v_ref[0, pl.ds(k0, n_chunk * LANES), vcols],
                                       preferred_element_type=F32)

    all_chunks = KEY_TILE // LANES
    last = N_ATTN_HEADS - 1

    def full_window(t, _):
        for h in range(N_ATTN_HEADS):
            if h < last:
                scores(t, h + 1, all_chunks)
            else:
                scores(t + 1, 0, all_chunks)
            update(t, h, 0, ATT_TILE, all_chunks, False)
        return 0

    scores(0, 0, all_chunks)
    lax.fori_loop(0, n_full, full_window, 0)

    for h in range(N_ATTN_HEADS):
        if h < last:
            scores(n_full, h + 1, all_chunks)
        for r0 in range(0, ATT_TILE, DIAG_BAND):
            update(n_full, h, r0, DIAG_BAND, (r0 + DIAG_BAND) // LANES, True)

    outs = []
    for p in range(N_PAIRS):
        o0 = acc_scr[2 * p, :, :LANES] / acc_scr[2 * p, :, LANES:]
        o1 = acc_scr[2 * p + 1, :, :LANES] / acc_scr[2 * p + 1, :, LANES:]
        outs.append(jnp.where(first, o0, o1))
    o_ref[0] = _rms(jnp.concatenate(outs, axis=1), g_ref[...]).astype(BF16)


def _fox_attention(q, k, v, ccol, crow, g):
    batch, seq, _ = q.shape
    n_win = seq // KEY_TILE
    stat = pltpu.VMEM((N_ATTN_HEADS, ATT_TILE, LANES), F32)
    return pl.pallas_call(
        _fox_kernel,
        out_shape=jax.ShapeDtypeStruct((batch, seq, D_ATTN), BF16),
        grid=(batch, seq // ATT_TILE),
        in_specs=[pl.BlockSpec((1, ATT_TILE, D_ATTN), lambda b, i: (b, i, 0)),
                  pl.BlockSpec((1, seq, D_ATTN), lambda b, i: (b, 0, 0)),
                  pl.BlockSpec((1, seq, 2 * D_ATTN), lambda b, i: (b, 0, 0)),
                  pl.BlockSpec((1, ATT_TILE, N_ATTN_HEADS), lambda b, i: (b, i, 0)),
                  pl.BlockSpec((1, n_win, N_ATTN_HEADS, KEY_TILE), lambda b, i: (b, 0, 0, 0)),
                  _const_spec((1, D_ATTN))],
        out_specs=pl.BlockSpec((1, ATT_TILE, D_ATTN), lambda b, i: (b, i, 0)),
        scratch_shapes=[pltpu.VMEM((N_ATTN_HEADS, ATT_TILE, LANES), BF16), stat, stat,
                        pltpu.VMEM((N_ATTN_HEADS, ATT_TILE, 2 * LANES), F32),
                        pltpu.VMEM((N_ATTN_HEADS, ATT_TILE, KEY_TILE), F32)],
        compiler_params=pltpu.CompilerParams(
            dimension_semantics=("arbitrary", "arbitrary"), vmem_limit_bytes=VMEM_LIMIT),
        name="fox_attn",
    )(q, k, v, ccol, crow, g)


def _mix_out_kernel(x_ref, a_ref, u_ref, wout_ref, xg_ref, wq_ref, mk_ref, mv_ref, wo_ref,
                    o_ref, o_scr):
    y = jnp.dot(a_ref[0], wout_ref[0, :D_ATTN, :].astype(BF16), preferred_element_type=F32)
    y = y + jnp.dot(u_ref[0], wout_ref[0, D_ATTN:, :].astype(BF16), preferred_element_type=F32)
    x = x_ref[0] + y

    hq = _rms(x, xg_ref[...]).astype(BF16)
    q = jnp.dot(hq, wq_ref[0].astype(BF16), preferred_element_type=F32) * (1.0 / math.sqrt(XATTN_HEAD_DIM))
    q = q.astype(BF16)
    for h in range(N_XATTN_HEADS):
        cols = slice(h * XATTN_HEAD_DIM, (h + 1) * XATTN_HEAD_DIM)
        s = lax.dot_general(q[:, cols], mk_ref[0, 0, :, cols], (((1,), (1,)), ((), ())),
                            preferred_element_type=F32)
        p = jnp.exp(s - jnp.max(s, axis=-1, keepdims=True))
        p = p / jnp.sum(p, axis=-1, keepdims=True)
        o_scr[:, cols] = jnp.dot(p.astype(BF16), mv_ref[0, 0, :, cols],
                                 preferred_element_type=F32).astype(BF16)
    o_ref[0] = x + jnp.dot(o_scr[...], wo_ref[0].astype(BF16), preferred_element_type=F32)


def _mix_out(x, attn, u, wout, xg, wq, mk, mv, wo, layer):
    batch, seq, _ = x.shape
    tok = lambda w: pl.BlockSpec((1, OUT_TILE, w), lambda b, s: (b, s, 0))
    mem_spec = pl.BlockSpec((1, 1, N_MEM, D_MODEL), lambda b, s: (layer, b, 0, 0))
    return pl.pallas_call(
        _mix_out_kernel,
        out_shape=jax.ShapeDtypeStruct((batch, seq, D_MODEL), F32),
        grid=(batch, seq // OUT_TILE),
        in_specs=[tok(D_MODEL), tok(D_ATTN), tok(D_CONV),
                  _layer_spec((D_MODEL, D_MODEL), layer), _const_spec((1, D_MODEL)),
                  _layer_spec((D_MODEL, D_MODEL), layer), mem_spec, mem_spec,
                  _layer_spec((D_MODEL, D_MODEL), layer)],
        out_specs=tok(D_MODEL),
        scratch_shapes=[pltpu.VMEM((OUT_TILE, D_MODEL), BF16)],
        compiler_params=pltpu.CompilerParams(
            dimension_semantics=("arbitrary", "arbitrary"), vmem_limit_bytes=VMEM_LIMIT),
        name="mix_out",
    )(x, attn, u, wout, xg, wq, mk, mv, wo)


def kernel(x, mem, ffn1_norm_g, ffn1_w_gate, ffn1_w_up, ffn1_w_down, mix_norm_g, w_in, b_f, conv_w, conv_b, conv_ln_g, conv_ln_b, attn_out_g, conv_out_g, w_out, xattn_norm_g, mem_norm_g, xattn_w_q, xattn_w_kv, xattn_w_o, ffn2_norm_g, ffn2_w_gate, ffn2_w_up, ffn2_w_down, final_norm_g):
    batch, seq, _ = x.shape
    depth = w_in.shape[0]
    n_win = seq // KEY_TILE
    row = lambda v: v.reshape(1, -1)

    mem_k, mem_v = _memkv(mem, mem_norm_g.reshape(depth, 1, D_MODEL), xattn_w_kv)
    fg = row(final_norm_g)
    tri = jnp.triu(jnp.ones((TOK_TILE, TOK_TILE), BF16))
    w_in_t = jnp.swapaxes(w_in, 1, 2)

    for l in range(depth):
        x2d = _ffn(x.reshape(batch * seq, D_MODEL), row(ffn1_norm_g[l]), ffn1_w_gate, ffn1_w_up,
                   ffn1_w_down, fg, l, final=False)
        x = x2d.reshape(batch, seq, D_MODEL)

        cw = jnp.broadcast_to(conv_w[l][:, None, :], (CONV_WIDTH, SUBLANES, D_CONV))
        q, k, v, c, u = _mix_in(
            x, row(mix_norm_g[l]), w_in_t, b_f[l].reshape(N_ATTN_HEADS, 1), tri, cw, row(conv_b[l]),
            row(conv_ln_g[l]), row(conv_ln_b[l]), row(conv_out_g[l]), l)

        ccol = c.transpose(0, 2, 1)
        crow = c.reshape(batch, N_ATTN_HEADS, n_win, KEY_TILE).transpose(0, 2, 1, 3)
        attn = _fox_attention(q, k, v, ccol, crow, row(attn_out_g[l]))

        x = _mix_out(x, attn, u, w_out, row(xattn_norm_g[l]), xattn_w_q, mem_k, mem_v, xattn_w_o, l)

        last = l == depth - 1
        x2d = _ffn(x.reshape(batch * seq, D_MODEL), row(ffn2_norm_g[l]), ffn2_w_gate, ffn2_w_up,
                   ffn2_w_down, fg, l, final=last)
        x = x2d.reshape(batch, seq, D_MODEL)
    return x
```

```python
import functools
import math

import jax
import jax.numpy as jnp
from jax import lax
from jax.experimental import pallas as pl
from jax.experimental.pallas import tpu as pltpu

F32 = jnp.float32
BF16 = jnp.bfloat16

D_MODEL = 1024
N_MEM = 256
D_ATTN = 512
D_CONV = 512
HEAD_DIM = 64
N_ATTN_HEADS = 8
CONV_WIDTH = 31
N_XATTN_HEADS = 4
XATTN_HEAD_DIM = 256
D_FF = 2816
D_IN = 3 * D_ATTN + N_ATTN_HEADS + 2 * D_CONV
EPS = 1e-6
NEG_INF = -1e30
LOG2E = math.log2(math.e)

LANES = 128
SUBLANES = 8
MXU_DIM = 256
VMEM_LIMIT = 56 * 1024 * 1024

TOK_TILE = 512
ATT_TILE = 512
KEY_TILE = ATT_TILE
DIAG_BAND = 256
FF_CHUNK = MXU_DIM
CONV_HALO = 32
CONV_ROWS = 64
OUT_TILE = 1024
MEM_BATCH = 4
N_PAIRS = D_ATTN // LANES


def _rms(x, g):
    return x * lax.rsqrt(jnp.mean(x * x, axis=-1, keepdims=True) + EPS) * g


def _const_spec(shape):
    nd = len(shape)
    return pl.BlockSpec(shape, lambda *_: (0,) * nd, pipeline_mode=pl.Buffered(1))


def _layer_spec(shape, layer):
    nd = len(shape)
    return pl.BlockSpec((1,) + shape, lambda *_: (layer,) + (0,) * nd, pipeline_mode=pl.Buffered(1))


def _ffn_kernel(x_ref, xn_ref, g_ref, wg_ref, wu_ref, wd_ref, fg_ref, o_ref,
                h_scr, hn_scr, act_scr, act0_scr, *, final):
    def hidden_chunk(h_ref, c):
        cols = slice(c * FF_CHUNK, (c + 1) * FF_CHUNK)
        h = h_ref[...]
        gate = jnp.dot(h, wg_ref[0, :, cols].astype(BF16), preferred_element_type=F32)
        up = jnp.dot(h, wu_ref[0, :, cols].astype(BF16), preferred_element_type=F32)
        return (gate * jax.nn.sigmoid(gate) * up).astype(BF16)

    @pl.when(pl.program_id(0) == 0)
    def _():
        hn_scr[...] = _rms(x_ref[...], g_ref[...]).astype(BF16)
        act0_scr[...] = hidden_chunk(hn_scr, 0)

    h_scr[...] = hn_scr[...]
    act_scr[:, :FF_CHUNK] = act0_scr[...]
    for c in range(1, D_FF // FF_CHUNK):
        act_scr[:, c * FF_CHUNK:(c + 1) * FF_CHUNK] = hidden_chunk(h_scr, c)
    hn_scr[...] = _rms(xn_ref[...], g_ref[...]).astype(BF16)
    y = jnp.dot(act_scr[...], wd_ref[0].astype(BF16), preferred_element_type=F32)
    act0_scr[...] = hidden_chunk(hn_scr, 0)
    out = x_ref[...] + 0.5 * y
    if final:
        out = _rms(out, fg_ref[...])
    o_ref[...] = out


def _ffn(x2d, g, wg, wu, wd, fg, layer, *, final):
    t = x2d.shape[0]
    n_tiles = t // TOK_TILE
    tile = pl.BlockSpec((TOK_TILE, D_MODEL), lambda i: (i, 0))
    next_tile = pl.BlockSpec((TOK_TILE, D_MODEL), lambda i: (jnp.minimum(i + 1, n_tiles - 1), 0))
    return pl.pallas_call(
        functools.partial(_ffn_kernel, final=final),
        out_shape=jax.ShapeDtypeStruct((t, D_MODEL), F32),
        grid=(n_tiles,),
        in_specs=[tile, next_tile, _const_spec((1, D_MODEL)), _layer_spec((D_MODEL, D_FF), layer),
                  _layer_spec((D_MODEL, D_FF), layer), _layer_spec((D_FF, D_MODEL), layer),
                  _const_spec((1, D_MODEL))],
        out_specs=tile,
        scratch_shapes=[pltpu.VMEM((TOK_TILE, D_MODEL), BF16), pltpu.VMEM((TOK_TILE, D_MODEL), BF16),
                        pltpu.VMEM((TOK_TILE, D_FF), BF16), pltpu.VMEM((TOK_TILE, FF_CHUNK), BF16)],
        compiler_params=pltpu.CompilerParams(
            dimension_semantics=("arbitrary",), vmem_limit_bytes=VMEM_LIMIT),
        name="ffn_final" if final else "ffn",
    )(x2d, x2d, g, wg, wu, wd, fg)


def _memkv_kernel(mem_ref, g_ref, w_ref, k_ref, v_ref):
    mem = mem_ref[...].reshape(MEM_BATCH * N_MEM, D_MODEL)
    mn = _rms(mem, g_ref[0]).astype(BF16)
    kv = jnp.dot(mn, w_ref[0].astype(BF16), preferred_element_type=F32)
    k_ref[0] = kv[:, :D_MODEL].astype(BF16).reshape(MEM_BATCH, N_MEM, D_MODEL)
    v_ref[0] = kv[:, D_MODEL:].astype(BF16).reshape(MEM_BATCH, N_MEM, D_MODEL)


def _memkv(mem, g, w_kv):
    depth, batch = g.shape[0], mem.shape[0]
    out = jax.ShapeDtypeStruct((depth, batch, N_MEM, D_MODEL), BF16)
    out_spec = pl.BlockSpec((1, MEM_BATCH, N_MEM, D_MODEL), lambda l, b: (l, b, 0, 0))
    return pl.pallas_call(
        _memkv_kernel,
        out_shape=(out, out),
        grid=(depth, batch // MEM_BATCH),
        in_specs=[pl.BlockSpec((MEM_BATCH, N_MEM, D_MODEL), lambda l, b: (b, 0, 0)),
                  pl.BlockSpec((1, 1, D_MODEL), lambda l, b: (l, 0, 0)),
                  pl.BlockSpec((1, D_MODEL, 2 * D_MODEL), lambda l, b: (l, 0, 0))],
        out_specs=(out_spec, out_spec),
        compiler_params=pltpu.CompilerParams(
            dimension_semantics=("arbitrary", "arbitrary"), vmem_limit_bytes=VMEM_LIMIT),
        name="mem_kv",
    )(mem, g, w_kv)


def _log_sigmoid(z):
    return jnp.minimum(z, 0.0) - jnp.log1p(jnp.exp(-jnp.abs(z)))


def _split3(a):
    hi = a.astype(BF16).astype(F32)
    r = a - hi
    mid = r.astype(BF16).astype(F32)
    lo = (r - mid).astype(BF16).astype(F32)
    return hi, mid, lo


def _mix_in_kernel(x_ref, g_ref, w_ref, bf_ref, tri_ref, cw_ref, cb_ref, lng_ref,
                   lnb_ref, cog_ref, q_ref, k_ref, v_ref, c_ref, u_ref,
                   wqkv_scr, wf_scr, wag_scr, h_scr, ubuf, shifted, carry):
    s_idx = pl.program_id(1)

    @pl.when(jnp.logical_and(pl.program_id(0) == 0, s_idx == 0))
    def _():
        f0 = 3 * D_ATTN
        wqkv_scr[...] = jnp.transpose(w_ref[0, :f0, :]).astype(BF16)
        wf_scr[...] = jnp.transpose(w_ref[0, f0:f0 + LANES, :]).astype(BF16)
        wag_scr[...] = jnp.transpose(w_ref[0, f0 + N_ATTN_HEADS:, :]).astype(BF16)

    def project(w_scr):
        return jnp.dot(h_scr[...], w_scr[...], preferred_element_type=F32)

    @pl.when(s_idx == 0)
    def _():
        carry[...] = jnp.zeros_like(carry)
        ubuf[0:CONV_HALO, :] = jnp.zeros((CONV_HALO, D_CONV), F32)

    h_scr[...] = _rms(x_ref[0], g_ref[...]).astype(BF16)

    ag = project(wag_scr)
    ubuf[CONV_HALO:, :] = ag[:, :D_CONV] * jax.nn.sigmoid(ag[:, D_CONV:])
    n_shift = TOK_TILE + CONV_HALO - SUBLANES
    for r in range(1, SUBLANES):
        shifted[r - 1] = ubuf[r:r + n_shift, :]

    qkv = project(wqkv_scr)
    q_ref[0] = (qkv[:, :D_ATTN] * (LOG2E / math.sqrt(HEAD_DIM))).astype(BF16)
    k_ref[0] = qkv[:, D_ATTN:2 * D_ATTN].astype(BF16)
    ones = jnp.ones((TOK_TILE, LANES), BF16)
    for p in range(N_PAIRS):
        v_ref[0, :, 2 * p * LANES:(2 * p + 1) * LANES] = (
            qkv[:, 2 * D_ATTN + p * LANES:2 * D_ATTN + (p + 1) * LANES].astype(BF16))
        v_ref[0, :, (2 * p + 1) * LANES:(2 * p + 2) * LANES] = ones

    f_t = jnp.transpose(project(wf_scr))
    logf = _log_sigmoid(f_t[:N_ATTN_HEADS, :] + bf_ref[...]) * LOG2E
    parts = jnp.concatenate(_split3(logf) + (jnp.zeros_like(logf),), axis=0).astype(BF16)
    sums = jnp.dot(parts, tri_ref[...], preferred_element_type=F32)
    h8 = N_ATTN_HEADS
    c_tile = carry[...] + sums[:h8] + sums[h8:2 * h8] + sums[2 * h8:3 * h8]
    c_ref[0] = c_tile
    carry[...] = c_tile[:, TOK_TILE - 1:TOK_TILE]

    for rb in range(TOK_TILE // CONV_ROWS):
        r0 = rb * CONV_ROWS
        acc = jnp.zeros((CONV_ROWS // SUBLANES, SUBLANES, D_CONV), F32)
        for tap in range(CONV_WIDTH):
            off = r0 + tap + CONV_HALO - (CONV_WIDTH - 1)
            sh, al = off % SUBLANES, (off // SUBLANES) * SUBLANES
            if sh == 0:
                blk = ubuf[al:al + CONV_ROWS, :]
            else:
                blk = shifted[sh - 1, al:al + CONV_ROWS, :]
            acc = acc + blk.reshape(CONV_ROWS // SUBLANES, SUBLANES, D_CONV) * cw_ref[tap]
        y = acc.reshape(CONV_ROWS, D_CONV) + cb_ref[...]
        mu = jnp.mean(y, axis=-1, keepdims=True)
        yc = y - mu
        yn = yc * lax.rsqrt(jnp.mean(yc * yc, axis=-1, keepdims=True) + EPS) * lng_ref[...] + lnb_ref[...]
        sw = yn * jax.nn.sigmoid(yn)
        u_ref[0, r0:r0 + CONV_ROWS, :] = _rms(sw, cog_ref[...]).astype(BF16)

    ubuf[0:CONV_HALO, :] = ubuf[TOK_TILE:TOK_TILE + CONV_HALO, :]


def _mix_in(x, g, w, bf, tri, cw, cb, lng, lnb, cog, layer):
    batch, seq, _ = x.shape
    tok = lambda w: pl.BlockSpec((1, TOK_TILE, w), lambda b, s: (b, s, 0))
    n_shift = TOK_TILE + CONV_HALO - SUBLANES
    outs = (jax.ShapeDtypeStruct((batch, seq, D_ATTN), BF16),) * 2 + (
        jax.ShapeDtypeStruct((batch, seq, 2 * D_ATTN), BF16),
        jax.ShapeDtypeStruct((batch, N_ATTN_HEADS, seq), F32),
        jax.ShapeDtypeStruct((batch, seq, D_CONV), BF16))
    return pl.pallas_call(
        _mix_in_kernel,
        out_shape=outs,
        grid=(batch, seq // TOK_TILE),
        in_specs=[tok(D_MODEL), _const_spec((1, D_MODEL)), _layer_spec((D_IN, D_MODEL), layer),
                  _const_spec((N_ATTN_HEADS, 1)), _const_spec((TOK_TILE, TOK_TILE)),
                  _const_spec((CONV_WIDTH, SUBLANES, D_CONV)),
                  _const_spec((1, D_CONV)), _const_spec((1, D_CONV)), _const_spec((1, D_CONV)),
                  _const_spec((1, D_CONV))],
        out_specs=(tok(D_ATTN), tok(D_ATTN), tok(2 * D_ATTN),
                   pl.BlockSpec((1, N_ATTN_HEADS, TOK_TILE), lambda b, s: (b, 0, s)), tok(D_CONV)),
        scratch_shapes=[pltpu.VMEM((D_MODEL, 3 * D_ATTN), BF16), pltpu.VMEM((D_MODEL, LANES), BF16),
                        pltpu.VMEM((D_MODEL, 2 * D_CONV), BF16),
                        pltpu.VMEM((TOK_TILE, D_MODEL), BF16),
                        pltpu.VMEM((TOK_TILE + CONV_HALO, D_CONV), F32),
                        pltpu.VMEM((SUBLANES - 1, n_shift, D_CONV), F32),
                        pltpu.VMEM((N_ATTN_HEADS, 1), F32)],
        compiler_params=pltpu.CompilerParams(
            dimension_semantics=("arbitrary", "arbitrary"), vmem_limit_bytes=VMEM_LIMIT),
        name="mix_in",
    )(x, g, w, bf, tri, cw, cb, lng, lnb, cog)


def _fox_kernel(q_ref, k_ref, v_ref, ccol_ref, crow_ref, g_ref, o_ref,
                qm_scr, ct_scr, m_scr, acc_scr, s_scr):
    i = pl.program_id(1)
    n_full = i
    lane = lax.broadcasted_iota(jnp.int32, (ATT_TILE, LANES), 1)
    first = lane < HEAD_DIM

    c_t = ccol_ref[0]
    for p in range(N_PAIRS):
        qp = q_ref[0, :, p * LANES:(p + 1) * LANES]
        zero = jnp.zeros_like(qp)
        qm_scr[2 * p] = jnp.where(first, qp, zero)
        qm_scr[2 * p + 1] = jnp.where(first, zero, qp)
    for h in range(N_ATTN_HEADS):
        ct_scr[h] = jnp.broadcast_to(c_t[:, h:h + 1], (ATT_TILE, LANES))

    def scores(t, h, n_chunk):
        k0 = pl.multiple_of(t * KEY_TILE, KEY_TILE)
        pair = slice((h // 2) * LANES, (h // 2 + 1) * LANES)
        c_s = crow_ref[0, t]
        s = lax.dot_general(qm_scr[h], k_ref[0, pl.ds(k0, n_chunk * LANES), pair],
                            (((1,), (1,)), ((), ())), preferred_element_type=F32)
        ct = ct_scr[h]
        for c in range(n_chunk):
            cols = slice(c * LANES, (c + 1) * LANES)
            s_scr[h, :, cols] = s[:, cols] + ct - c_s[h:h + 1, cols]

    def update(t, h, r0, n_rows, n_chunk, diagonal, first_window):
        k0 = pl.multiple_of(t * KEY_TILE, KEY_TILE)
        rows = slice(r0, r0 + n_rows)
        vcols = slice((h // 2) * 2 * LANES, (h // 2 + 1) * 2 * LANES)

        q_pos = lax.broadcasted_iota(jnp.int32, (n_rows, LANES), 0) + r0
        k_pos = lax.broadcasted_iota(jnp.int32, (n_rows, LANES), 1)

        def logits(c):
            sc = s_scr[h, rows, c * LANES:(c + 1) * LANES]
            if diagonal and (c + 1) * LANES > r0:
                sc = jnp.where(q_pos >= k_pos + c * LANES, sc, NEG_INF)
            return sc

        m_blk = functools.reduce(jnp.maximum, [logits(c) for c in range(n_chunk)])
        m_row = jnp.max(m_blk, axis=-1, keepdims=True)
        if first_window:
            m_new = jnp.broadcast_to(m_row, (n_rows, LANES))
        else:
            m_old = m_scr[h, rows, :]
            m_new = jnp.maximum(m_old, m_row)
            alpha = jnp.exp2(m_old - m_new)
            acc_scr[h, rows, :] = jnp.concatenate([alpha, alpha], axis=1) * acc_scr[h, rows, :]
        p_bf = jnp.concatenate([jnp.exp2(logits(c) - m_new) for c in range(n_chunk)], axis=1).astype(BF16)
        m_scr[h, rows, :] = m_new
        pv = jnp.dot(p_bf, v_ref[0, pl.ds(k0, n_chunk * LANES), vcols], preferred_element_type=F32)
        if first_window:
            acc_scr[h, rows, :] = pv
        else:
            acc_scr[h, rows, :] += pv

    all_chunks = KEY_TILE // LANES
    last = N_ATTN_HEADS - 1

    def full_window(t, first_window):
        for h in range(N_ATTN_HEADS):
            if h < last:
                scores(t, h + 1, all_chunks)
            else:
                scores(t + 1, 0, all_chunks)
            update(t, h, 0, ATT_TILE, all_chunks, False, first_window)

    def diagonal_window(first_window):
        for h in range(N_ATTN_HEADS):
            if h < last:
                scores(n_full, h + 1, all_chunks)
            for r0 in range(0, ATT_TILE, DIAG_BAND):
                update(n_full, h, r0, DIAG_BAND, (r0 + DIAG_BAND) // LANES, True, first_window)

    scores(0, 0, all_chunks)

    @pl.when(i == 0)
    def _():
        diagonal_window(True)

    @pl.when(i > 0)
    def _():
        full_window(0, True)
        lax.fori_loop(1, n_full, lambda t, c: (full_window(t, False), c)[1], 0)
        diagonal_window(False)

    outs = []
    for p in range(N_PAIRS):
        o0 = acc_scr[2 * p, :, :LANES] / acc_scr[2 * p, :, LANES:]
        o1 = acc_scr[2 * p + 1, :, :LANES] / acc_scr[2 * p + 1, :, LANES:]
        outs.append(jnp.where(first, o0, o1))
    o_ref[0] = _rms(jnp.concatenate(outs, axis=1), g_ref[...]).astype(BF16)


def _fox_attention(q, k, v, ccol, crow, g):
    batch, seq, _ = q.shape
    n_win = seq // KEY_TILE
    stat = pltpu.VMEM((N_ATTN_HEADS, ATT_TILE, LANES), F32)
    return pl.pallas_call(
        _fox_kernel,
        out_shape=jax.ShapeDtypeStruct((batch, seq, D_ATTN), BF16),
        grid=(batch, seq // ATT_TILE),
        in_specs=[pl.BlockSpec((1, ATT_TILE, D_ATTN), lambda b, i: (b, i, 0)),
                  pl.BlockSpec((1, seq, D_ATTN), lambda b, i: (b, 0, 0)),
                  pl.BlockSpec((1, seq, 2 * D_ATTN), lambda b, i: (b, 0, 0)),
                  pl.BlockSpec((1, ATT_TILE, N_ATTN_HEADS), lambda b, i: (b, i, 0)),
                  pl.BlockSpec((1, n_win, N_ATTN_HEADS, KEY_TILE), lambda b, i: (b, 0, 0, 0)),
                  _const_spec((1, D_ATTN))],
        out_specs=pl.BlockSpec((1, ATT_TILE, D_ATTN), lambda b, i: (b, i, 0)),
        scratch_shapes=[pltpu.VMEM((N_ATTN_HEADS, ATT_TILE, LANES), BF16), stat, stat,
                        pltpu.VMEM((N_ATTN_HEADS, ATT_TILE, 2 * LANES), F32),
                        pltpu.VMEM((N_ATTN_HEADS, ATT_TILE, KEY_TILE), F32)],
        compiler_params=pltpu.CompilerParams(
            dimension_semantics=("arbitrary", "arbitrary"), vmem_limit_bytes=VMEM_LIMIT),
        name="fox_attn",
    )(q, k, v, ccol, crow, g)


def _mix_out_kernel(x_ref, a_ref, u_ref, wout_ref, xg_ref, wq_ref, mk_ref, mv_ref, wo_ref,
                    o_ref, o_scr):
    y = jnp.dot(a_ref[0], wout_ref[0, :D_ATTN, :].astype(BF16), preferred_element_type=F32)
    y = y + jnp.dot(u_ref[0], wout_ref[0, D_ATTN:, :].astype(BF16), preferred_element_type=F32)
    x = x_ref[0] + y

    hq = _rms(x, xg_ref[...]).astype(BF16)
    q = jnp.dot(hq, wq_ref[0].astype(BF16), preferred_element_type=F32) * (1.0 / math.sqrt(XATTN_HEAD_DIM))
    q = q.astype(BF16)
    for h in range(N_XATTN_HEADS):
        cols = slice(h * XATTN_HEAD_DIM, (h + 1) * XATTN_HEAD_DIM)
        s = lax.dot_general(q[:, cols], mk_ref[0, 0, :, cols], (((1,), (1,)), ((), ())),
                            preferred_element_type=F32)
        p = jnp.exp(s - jnp.max(s, axis=-1, keepdims=True))
        p = p / jnp.sum(p, axis=-1, keepdims=True)
        o_scr[:, cols] = jnp.dot(p.astype(BF16), mv_ref[0, 0, :, cols],
                                 preferred_element_type=F32).astype(BF16)
    o_ref[0] = x + jnp.dot(o_scr[...], wo_ref[0].astype(BF16), preferred_element_type=F32)


def _mix_out(x, attn, u, wout, xg, wq, mk, mv, wo, layer):
    batch, seq, _ = x.shape
    tok = lambda w: pl.BlockSpec((1, OUT_TILE, w), lambda b, s: (b, s, 0))
    mem_spec = pl.BlockSpec((1, 1, N_MEM, D_MODEL), lambda b, s: (layer, b, 0, 0))
    return pl.pallas_call(
        _mix_out_kernel,
        out_shape=jax.ShapeDtypeStruct((batch, seq, D_MODEL), F32),
        grid=(batch, seq // OUT_TILE),
        in_specs=[tok(D_MODEL), tok(D_ATTN), tok(D_CONV),
                  _layer_spec((D_MODEL, D_MODEL), layer), _const_spec((1, D_MODEL)),
                  _layer_spec((D_MODEL, D_MODEL), layer), mem_spec, mem_spec,
                  _layer_spec((D_MODEL, D_MODEL), layer)],
        out_specs=tok(D_MODEL),
        scratch_shapes=[pltpu.VMEM((OUT_TILE, D_MODEL), BF16)],
        compiler_params=pltpu.CompilerParams(
            dimension_semantics=("arbitrary", "arbitrary"), vmem_limit_bytes=VMEM_LIMIT),
        name="mix_out",
    )(x, attn, u, wout, xg, wq, mk, mv, wo)


def kernel(x, mem, ffn1_norm_g, ffn1_w_gate, ffn1_w_up, ffn1_w_down, mix_norm_g, w_in, b_f, conv_w, conv_b, conv_ln_g, conv_ln_b, attn_out_g, conv_out_g, w_out, xattn_norm_g, mem_norm_g, xattn_w_q, xattn_w_kv, xattn_w_o, ffn2_norm_g, ffn2_w_gate, ffn2_w_up, ffn2_w_down, final_norm_g):
    batch, seq, _ = x.shape
    depth = w_in.shape[0]
    n_win = seq // KEY_TILE
    row = lambda v: v.reshape(1, -1)

    mem_k, mem_v = _memkv(mem, mem_norm_g.reshape(depth, 1, D_MODEL), xattn_w_kv)
    fg = row(final_norm_g)
    tri = jnp.triu(jnp.ones((TOK_TILE, TOK_TILE), BF16))
    w_in_t = jnp.swapaxes(w_in, 1, 2)

    for l in range(depth):
        x2d = _ffn(x.reshape(batch * seq, D_MODEL), row(ffn1_norm_g[l]), ffn1_w_gate, ffn1_w_up,
                   ffn1_w_down, fg, l, final=False)
        x = x2d.reshape(batch, seq, D_MODEL)

        cw = jnp.broadcast_to(conv_w[l][:, None, :], (CONV_WIDTH, SUBLANES, D_CONV))
        q, k, v, c, u = _mix_in(
            x, row(mix_norm_g[l]), w_in_t, b_f[l].reshape(N_ATTN_HEADS, 1), tri, cw, row(conv_b[l]),
            row(conv_ln_g[l]), row(conv_ln_b[l]), row(conv_out_g[l]), l)

        ccol = c.transpose(0, 2, 1)
        crow = c.reshape(batch, N_ATTN_HEADS, n_win, KEY_TILE).transpose(0, 2, 1, 3)
        attn = _fox_attention(q, k, v, ccol, crow, row(attn_out_g[l]))

        x = _mix_out(x, attn, u, w_out, row(xattn_norm_g[l]), xattn_w_q, mem_k, mem_v, xattn_w_o, l)

        last = l == depth - 1
        x2d = _ffn(x.reshape(batch * seq, D_MODEL), row(ffn2_norm_g[l]), ffn2_w_gate, ffn2_w_up,
                   ffn2_w_down, fg, l, final=last)
        x = x2d.reshape(batch, seq, D_MODEL)
    return x
```

```python
import functools
import math

import jax
import jax.numpy as jnp
import numpy as np
from jax import lax
from jax.experimental import pallas as pl
from jax.experimental.pallas import tpu as pltpu

F32 = jnp.float32
BF16 = jnp.bfloat16

D_MODEL = 1024
N_MEM = 256
D_ATTN = 512
D_CONV = 512
HEAD_DIM = 64
N_ATTN_HEADS = 8
CONV_WIDTH = 31
N_XATTN_HEADS = 4
XATTN_HEAD_DIM = 256
D_FF = 2816
D_IN = 3 * D_ATTN + N_ATTN_HEADS + 2 * D_CONV
EPS = 1e-6
NEG_INF = -1e30
LOG2E = math.log2(math.e)

LANES = 128
SUBLANES = 8
MXU_DIM = 256
VMEM_LIMIT = 56 * 1024 * 1024

TOK_TILE = 512
ATT_TILE = 512
KEY_TILE = ATT_TILE
DIAG_BAND = 256
FF_CHUNK = MXU_DIM
CONV_HALO = 32
CONV_ROWS = 64
OUT_TILE = 1024
MEM_BATCH = 4
N_PAIRS = D_ATTN // LANES


def _rms(x, g):
    return x * lax.rsqrt(jnp.mean(x * x, axis=-1, keepdims=True) + EPS) * g


def _const_spec(shape):
    nd = len(shape)
    return pl.BlockSpec(shape, lambda *_: (0,) * nd, pipeline_mode=pl.Buffered(1))


def _layer_spec(shape, layer):
    nd = len(shape)
    return pl.BlockSpec((1,) + shape, lambda *_: (layer,) + (0,) * nd, pipeline_mode=pl.Buffered(1))


def _ffn_kernel(x_ref, xn_ref, g_ref, wg_ref, wu_ref, wd_ref, fg_ref, o_ref,
                h_scr, hn_scr, act_scr, act0_scr, *, final):
    def hidden_chunk(h_ref, c):
        cols = slice(c * FF_CHUNK, (c + 1) * FF_CHUNK)
        h = h_ref[...]
        gate = jnp.dot(h, wg_ref[0, :, cols].astype(BF16), preferred_element_type=F32)
        up = jnp.dot(h, wu_ref[0, :, cols].astype(BF16), preferred_element_type=F32)
        return (gate * jax.nn.sigmoid(gate) * up).astype(BF16)

    @pl.when(pl.program_id(0) == 0)
    def _():
        hn_scr[...] = _rms(x_ref[...], g_ref[...]).astype(BF16)
        act0_scr[...] = hidden_chunk(hn_scr, 0)

    h_scr[...] = hn_scr[...]
    act_scr[:, :FF_CHUNK] = act0_scr[...]
    for c in range(1, D_FF // FF_CHUNK):
        act_scr[:, c * FF_CHUNK:(c + 1) * FF_CHUNK] = hidden_chunk(h_scr, c)
    hn_scr[...] = _rms(xn_ref[...], g_ref[...]).astype(BF16)
    y = jnp.dot(act_scr[...], wd_ref[0].astype(BF16), preferred_element_type=F32)
    act0_scr[...] = hidden_chunk(hn_scr, 0)
    out = x_ref[...] + 0.5 * y
    if final:
        out = _rms(out, fg_ref[...])
    o_ref[...] = out


def _ffn(x2d, g, wg, wu, wd, fg, layer, *, final):
    t = x2d.shape[0]
    n_tiles = t // TOK_TILE
    tile = pl.BlockSpec((TOK_TILE, D_MODEL), lambda i: (i, 0))
    next_tile = pl.BlockSpec((TOK_TILE, D_MODEL), lambda i: (jnp.minimum(i + 1, n_tiles - 1), 0))
    return pl.pallas_call(
        functools.partial(_ffn_kernel, final=final),
        out_shape=jax.ShapeDtypeStruct((t, D_MODEL), F32),
        grid=(n_tiles,),
        in_specs=[tile, next_tile, _const_spec((1, D_MODEL)), _layer_spec((D_MODEL, D_FF), layer),
                  _layer_spec((D_MODEL, D_FF), layer), _layer_spec((D_FF, D_MODEL), layer),
                  _const_spec((1, D_MODEL))],
        out_specs=tile,
        scratch_shapes=[pltpu.VMEM((TOK_TILE, D_MODEL), BF16), pltpu.VMEM((TOK_TILE, D_MODEL), BF16),
                        pltpu.VMEM((TOK_TILE, D_FF), BF16), pltpu.VMEM((TOK_TILE, FF_CHUNK), BF16)],
        compiler_params=pltpu.CompilerParams(
            dimension_semantics=("arbitrary",), vmem_limit_bytes=VMEM_LIMIT),
        name="ffn_final" if final else "ffn",
    )(x2d, x2d, g, wg, wu, wd, fg)


def _memkv_kernel(mem_ref, g_ref, w_ref, k_ref, v_ref):
    mem = mem_ref[...].reshape(MEM_BATCH * N_MEM, D_MODEL)
    mn = _rms(mem, g_ref[0]).astype(BF16)
    kv = jnp.dot(mn, w_ref[0].astype(BF16), preferred_element_type=F32)
    k_ref[0] = kv[:, :D_MODEL].astype(BF16).reshape(MEM_BATCH, N_MEM, D_MODEL)
    v_ref[0] = kv[:, D_MODEL:].astype(BF16).reshape(MEM_BATCH, N_MEM, D_MODEL)


def _memkv(mem, g, w_kv):
    depth, batch = g.shape[0], mem.shape[0]
    out = jax.ShapeDtypeStruct((depth, batch, N_MEM, D_MODEL), BF16)
    out_spec = pl.BlockSpec((1, MEM_BATCH, N_MEM, D_MODEL), lambda l, b: (l, b, 0, 0))
    return pl.pallas_call(
        _memkv_kernel,
        out_shape=(out, out),
        grid=(depth, batch // MEM_BATCH),
        in_specs=[pl.BlockSpec((MEM_BATCH, N_MEM, D_MODEL), lambda l, b: (b, 0, 0)),
                  pl.BlockSpec((1, 1, D_MODEL), lambda l, b: (l, 0, 0)),
                  pl.BlockSpec((1, D_MODEL, 2 * D_MODEL), lambda l, b: (l, 0, 0))],
        out_specs=(out_spec, out_spec),
        compiler_params=pltpu.CompilerParams(
            dimension_semantics=("arbitrary", "arbitrary"), vmem_limit_bytes=VMEM_LIMIT),
        name="mem_kv",
    )(mem, g, w_kv)


def _log_sigmoid(z):
    return jnp.minimum(z, 0.0) - jnp.log1p(jnp.exp(-jnp.abs(z)))


def _split3(a):
    hi = a.astype(BF16).astype(F32)
    r = a - hi
    mid = r.astype(BF16).astype(F32)
    lo = (r - mid).astype(BF16).astype(F32)
    return hi, mid, lo


def _mix_in_kernel(x_ref, g_ref, w_ref, bf_ref, tri_ref, cw_ref, cb_ref, lng_ref,
                   lnb_ref, cog_ref, q_ref, k_ref, v_ref, c_ref, u_ref,
                   wqkv_scr, wf_scr, wag_scr, h_scr, ubuf, shifted, carry):
    s_idx = pl.program_id(1)

    @pl.when(jnp.logical_and(pl.program_id(0) == 0, s_idx == 0))
    def _():
        f0 = 3 * D_ATTN
        wqkv_scr[...] = jnp.transpose(w_ref[0, :f0, :]).astype(BF16)
        wf_scr[...] = jnp.transpose(w_ref[0, f0:f0 + LANES, :]).astype(BF16)
        wag_scr[...] = jnp.transpose(w_ref[0, f0 + N_ATTN_HEADS:, :]).astype(BF16)

    def project(w_scr):
        return jnp.dot(h_scr[...], w_scr[...], preferred_element_type=F32)

    @pl.when(s_idx == 0)
    def _():
        carry[...] = jnp.zeros_like(carry)
        ubuf[0:CONV_HALO, :] = jnp.zeros((CONV_HALO, D_CONV), F32)

    h_scr[...] = _rms(x_ref[0], g_ref[...]).astype(BF16)

    ag = project(wag_scr)
    ubuf[CONV_HALO:, :] = ag[:, :D_CONV] * jax.nn.sigmoid(ag[:, D_CONV:])
    n_shift = TOK_TILE + CONV_HALO - SUBLANES
    for r in range(1, SUBLANES):
        shifted[r - 1] = ubuf[r:r + n_shift, :]

    qkv = project(wqkv_scr)
    q_ref[0] = (qkv[:, :D_ATTN] * (LOG2E / math.sqrt(HEAD_DIM))).astype(BF16)
    k_ref[0] = qkv[:, D_ATTN:2 * D_ATTN].astype(BF16)
    ones = jnp.ones((TOK_TILE, LANES), BF16)
    for p in range(N_PAIRS):
        v_ref[0, :, 2 * p * LANES:(2 * p + 1) * LANES] = (
            qkv[:, 2 * D_ATTN + p * LANES:2 * D_ATTN + (p + 1) * LANES].astype(BF16))
        v_ref[0, :, (2 * p + 1) * LANES:(2 * p + 2) * LANES] = ones

    f_t = jnp.transpose(project(wf_scr))
    logf = _log_sigmoid(f_t[:N_ATTN_HEADS, :] + bf_ref[...]) * LOG2E
    parts = jnp.concatenate(_split3(logf) + (jnp.zeros_like(logf),), axis=0).astype(BF16)
    sums = jnp.dot(parts, tri_ref[...], preferred_element_type=F32)
    h8 = N_ATTN_HEADS
    c_tile = carry[...] + sums[:h8] + sums[h8:2 * h8] + sums[2 * h8:3 * h8]
    c_ref[0] = c_tile
    carry[...] = c_tile[:, TOK_TILE - 1:TOK_TILE]

    for rb in range(TOK_TILE // CONV_ROWS):
        r0 = rb * CONV_ROWS
        acc = jnp.zeros((CONV_ROWS // SUBLANES, SUBLANES, D_CONV), F32)
        for tap in range(CONV_WIDTH):
            off = r0 + tap + CONV_HALO - (CONV_WIDTH - 1)
            sh, al = off % SUBLANES, (off // SUBLANES) * SUBLANES
            if sh == 0:
                blk = ubuf[al:al + CONV_ROWS, :]
            else:
                blk = shifted[sh - 1, al:al + CONV_ROWS, :]
            acc = acc + blk.reshape(CONV_ROWS // SUBLANES, SUBLANES, D_CONV) * cw_ref[tap]
        y = acc.reshape(CONV_ROWS, D_CONV) + cb_ref[...]
        mu = jnp.mean(y, axis=-1, keepdims=True)
        yc = y - mu
        yn = yc * lax.rsqrt(jnp.mean(yc * yc, axis=-1, keepdims=True) + EPS) * lng_ref[...] + lnb_ref[...]
        sw = yn * jax.nn.sigmoid(yn)
        u_ref[0, r0:r0 + CONV_ROWS, :] = _rms(sw, cog_ref[...]).astype(BF16)

    ubuf[0:CONV_HALO, :] = ubuf[TOK_TILE:TOK_TILE + CONV_HALO, :]


def _mix_in(x, g, w, bf, tri, cw, cb, lng, lnb, cog, layer):
    batch, seq, _ = x.shape
    tok = lambda w: pl.BlockSpec((1, TOK_TILE, w), lambda b, s: (b, s, 0))
    n_shift = TOK_TILE + CONV_HALO - SUBLANES
    outs = (jax.ShapeDtypeStruct((batch, seq, D_ATTN), BF16),) * 2 + (
        jax.ShapeDtypeStruct((batch, seq, 2 * D_ATTN), BF16),
        jax.ShapeDtypeStruct((batch, N_ATTN_HEADS, seq), F32),
        jax.ShapeDtypeStruct((batch, seq, D_CONV), BF16))
    return pl.pallas_call(
        _mix_in_kernel,
        out_shape=outs,
        grid=(batch, seq // TOK_TILE),
        in_specs=[tok(D_MODEL), _const_spec((1, D_MODEL)), _layer_spec((D_IN, D_MODEL), layer),
                  _const_spec((N_ATTN_HEADS, 1)), _const_spec((TOK_TILE, TOK_TILE)),
                  _const_spec((CONV_WIDTH, SUBLANES, D_CONV)),
                  _const_spec((1, D_CONV)), _const_spec((1, D_CONV)), _const_spec((1, D_CONV)),
                  _const_spec((1, D_CONV))],
        out_specs=(tok(D_ATTN), tok(D_ATTN), tok(2 * D_ATTN),
                   pl.BlockSpec((1, N_ATTN_HEADS, TOK_TILE), lambda b, s: (b, 0, s)), tok(D_CONV)),
        scratch_shapes=[pltpu.VMEM((D_MODEL, 3 * D_ATTN), BF16), pltpu.VMEM((D_MODEL, LANES), BF16),
                        pltpu.VMEM((D_MODEL, 2 * D_CONV), BF16),
                        pltpu.VMEM((TOK_TILE, D_MODEL), BF16),
                        pltpu.VMEM((TOK_TILE + CONV_HALO, D_CONV), F32),
                        pltpu.VMEM((SUBLANES - 1, n_shift, D_CONV), F32),
                        pltpu.VMEM((N_ATTN_HEADS, 1), F32)],
        compiler_params=pltpu.CompilerParams(
            dimension_semantics=("arbitrary", "arbitrary"), vmem_limit_bytes=VMEM_LIMIT),
        name="mix_in",
    )(x, g, w, bf, tri, cw, cb, lng, lnb, cog)


def _fox_kernel(q_ref, k_ref, v_ref, cq_ref, crow_ref, rep_ref, g_ref, o_ref,
                qm_scr, ct_scr, m_scr, acc_scr, s_scr):
    i = pl.program_id(1)
    n_full = i
    lane = lax.broadcasted_iota(jnp.int32, (ATT_TILE, LANES), 1)
    first = lane < HEAD_DIM

    for p in range(N_PAIRS):
        qp = q_ref[0, :, p * LANES:(p + 1) * LANES]
        zero = jnp.zeros_like(qp)
        qm_scr[2 * p] = jnp.where(first, qp, zero)
        qm_scr[2 * p + 1] = jnp.where(first, zero, qp)
    parts = jnp.concatenate(_split3(cq_ref[0]) + (jnp.zeros((N_ATTN_HEADS, ATT_TILE), F32),), axis=0)
    c_t = lax.dot_general(parts.astype(BF16), rep_ref[...], (((0,), (0,)), ((), ())),
                          preferred_element_type=F32)
    for h in range(N_ATTN_HEADS):
        ct_scr[h] = c_t[:, h * LANES:(h + 1) * LANES]

    def scores(t, h, n_chunk):
        k0 = pl.multiple_of(t * KEY_TILE, KEY_TILE)
        pair = slice((h // 2) * LANES, (h // 2 + 1) * LANES)
        c_s = crow_ref[0, t]
        s = lax.dot_general(qm_scr[h], k_ref[0, pl.ds(k0, n_chunk * LANES), pair],
                            (((1,), (1,)), ((), ())), preferred_element_type=F32)
        ct = ct_scr[h]
        for c in range(n_chunk):
            cols = slice(c * LANES, (c + 1) * LANES)
            s_scr[h, :, cols] = s[:, cols] + ct - c_s[h:h + 1, cols]

    def update(t, h, r0, n_rows, n_chunk, diagonal, first_window):
        k0 = pl.multiple_of(t * KEY_TILE, KEY_TILE)
        rows = slice(r0, r0 + n_rows)
        vcols = slice((h // 2) * 2 * LANES, (h // 2 + 1) * 2 * LANES)

        q_pos = lax.broadcasted_iota(jnp.int32, (n_rows, LANES), 0) + r0
        k_pos = lax.broadcasted_iota(jnp.int32, (n_rows, LANES), 1)

        def logits(c):
            sc = s_scr[h, rows, c * LANES:(c + 1) * LANES]
            if diagonal and (c + 1) * LANES > r0:
                sc = jnp.where(q_pos >= k_pos + c * LANES, sc, NEG_INF)
            return sc

        m_blk = functools.reduce(jnp.maximum, [logits(c) for c in range(n_chunk)])
        m_row = jnp.max(m_blk, axis=-1, keepdims=True)
        if first_window:
            m_new = jnp.broadcast_to(m_row, (n_rows, LANES))
        else:
            m_old = m_scr[h, rows, :]
            m_new = jnp.maximum(m_old, m_row)
            alpha = jnp.exp2(m_old - m_new)
            acc_scr[h, rows, :] = jnp.concatenate([alpha, alpha], axis=1) * acc_scr[h, rows, :]
        p_bf = jnp.concatenate([jnp.exp2(logits(c) - m_new) for c in range(n_chunk)], axis=1).astype(BF16)
        m_scr[h, rows, :] = m_new
        pv = jnp.dot(p_bf, v_ref[0, pl.ds(k0, n_chunk * LANES), vcols], preferred_element_type=F32)
        if first_window:
            acc_scr[h, rows, :] = pv
        else:
            acc_scr[h, rows, :] += pv

    all_chunks = KEY_TILE // LANES
    last = N_ATTN_HEADS - 1

    def full_window(t, first_window):
        for h in range(N_ATTN_HEADS):
            if h < last:
                scores(t, h + 1, all_chunks)
            else:
                scores(t + 1, 0, all_chunks)
            update(t, h, 0, ATT_TILE, all_chunks, False, first_window)

    def diagonal_window(first_window):
        for h in range(N_ATTN_HEADS):
            if h < last:
                scores(n_full, h + 1, all_chunks)
            for r0 in range(0, ATT_TILE, DIAG_BAND):
                update(n_full, h, r0, DIAG_BAND, (r0 + DIAG_BAND) // LANES, True, first_window)

    scores(0, 0, all_chunks)

    @pl.when(i == 0)
    def _():
        diagonal_window(True)

    @pl.when(i > 0)
    def _():
        full_window(0, True)
        lax.fori_loop(1, n_full, lambda t, c: (full_window(t, False), c)[1], 0)
        diagonal_window(False)

    outs = []
    for p in range(N_PAIRS):
        o0 = acc_scr[2 * p, :, :LANES] / acc_scr[2 * p, :, LANES:]
        o1 = acc_scr[2 * p + 1, :, :LANES] / acc_scr[2 * p + 1, :, LANES:]
        outs.append(jnp.where(first, o0, o1))
    o_ref[0] = _rms(jnp.concatenate(outs, axis=1), g_ref[...]).astype(BF16)


def _fox_attention(q, k, v, c, crow, rep, g):
    batch, seq, _ = q.shape
    n_win = seq // KEY_TILE
    stat = pltpu.VMEM((N_ATTN_HEADS, ATT_TILE, LANES), F32)
    return pl.pallas_call(
        _fox_kernel,
        out_shape=jax.ShapeDtypeStruct((batch, seq, D_ATTN), BF16),
        grid=(batch, seq // ATT_TILE),
        in_specs=[pl.BlockSpec((1, ATT_TILE, D_ATTN), lambda b, i: (b, i, 0)),
                  pl.BlockSpec((1, seq, D_ATTN), lambda b, i: (b, 0, 0)),
                  pl.BlockSpec((1, seq, 2 * D_ATTN), lambda b, i: (b, 0, 0)),
                  pl.BlockSpec((1, N_ATTN_HEADS, ATT_TILE), lambda b, i: (b, 0, i)),
                  pl.BlockSpec((1, n_win, N_ATTN_HEADS, KEY_TILE), lambda b, i: (b, 0, 0, 0)),
                  _const_spec((4 * N_ATTN_HEADS, N_ATTN_HEADS * LANES)), _const_spec((1, D_ATTN))],
        out_specs=pl.BlockSpec((1, ATT_TILE, D_ATTN), lambda b, i: (b, i, 0)),
        scratch_shapes=[pltpu.VMEM((N_ATTN_HEADS, ATT_TILE, LANES), BF16), stat, stat,
                        pltpu.VMEM((N_ATTN_HEADS, ATT_TILE, 2 * LANES), F32),
                        pltpu.VMEM((N_ATTN_HEADS, ATT_TILE, KEY_TILE), F32)],
        compiler_params=pltpu.CompilerParams(
            dimension_semantics=("arbitrary", "arbitrary"), vmem_limit_bytes=VMEM_LIMIT),
        name="fox_attn",
    )(q, k, v, c, crow, rep, g)


def _mix_out_kernel(x_ref, a_ref, u_ref, wout_ref, xg_ref, wq_ref, mk_ref, mv_ref, wo_ref,
                    o_ref, o_scr):
    y = jnp.dot(a_ref[0], wout_ref[0, :D_ATTN, :].astype(BF16), preferred_element_type=F32)
    y = y + jnp.dot(u_ref[0], wout_ref[0, D_ATTN:, :].astype(BF16), preferred_element_type=F32)
    x = x_ref[0] + y

    hq = _rms(x, xg_ref[...]).astype(BF16)
    q = jnp.dot(hq, wq_ref[0].astype(BF16), preferred_element_type=F32) * (1.0 / math.sqrt(XATTN_HEAD_DIM))
    q = q.astype(BF16)
    for h in range(N_XATTN_HEADS):
        cols = slice(h * XATTN_HEAD_DIM, (h + 1) * XATTN_HEAD_DIM)
        s = lax.dot_general(q[:, cols], mk_ref[0, 0, :, cols], (((1,), (1,)), ((), ())),
                            preferred_element_type=F32)
        p = jnp.exp(s - jnp.max(s, axis=-1, keepdims=True))
        p = p / jnp.sum(p, axis=-1, keepdims=True)
        o_scr[:, cols] = jnp.dot(p.astype(BF16), mv_ref[0, 0, :, cols],
                                 preferred_element_type=F32).astype(BF16)
    o_ref[0] = x + jnp.dot(o_scr[...], wo_ref[0].astype(BF16), preferred_element_type=F32)


def _mix_out(x, attn, u, wout, xg, wq, mk, mv, wo, layer):
    batch, seq, _ = x.shape
    tok = lambda w: pl.BlockSpec((1, OUT_TILE, w), lambda b, s: (b, s, 0))
    mem_spec = pl.BlockSpec((1, 1, N_MEM, D_MODEL), lambda b, s: (layer, b, 0, 0))
    return pl.pallas_call(
        _mix_out_kernel,
        out_shape=jax.ShapeDtypeStruct((batch, seq, D_MODEL), F32),
        grid=(batch, seq // OUT_TILE),
        in_specs=[tok(D_MODEL), tok(D_ATTN), tok(D_CONV),
                  _layer_spec((D_MODEL, D_MODEL), layer), _const_spec((1, D_MODEL)),
                  _layer_spec((D_MODEL, D_MODEL), layer), mem_spec, mem_spec,
                  _layer_spec((D_MODEL, D_MODEL), layer)],
        out_specs=tok(D_MODEL),
        scratch_shapes=[pltpu.VMEM((OUT_TILE, D_MODEL), BF16)],
        compiler_params=pltpu.CompilerParams(
            dimension_semantics=("arbitrary", "arbitrary"), vmem_limit_bytes=VMEM_LIMIT),
        name="mix_out",
    )(x, attn, u, wout, xg, wq, mk, mv, wo)


def _gate_replicator():
    mat = np.zeros((4 * N_ATTN_HEADS, N_ATTN_HEADS * LANES), np.float32)
    for h in range(N_ATTN_HEADS):
        for part in range(3):
            mat[part * N_ATTN_HEADS + h, h * LANES:(h + 1) * LANES] = 1.0
    return mat


def kernel(x, mem, ffn1_norm_g, ffn1_w_gate, ffn1_w_up, ffn1_w_down, mix_norm_g, w_in, b_f, conv_w, conv_b, conv_ln_g, conv_ln_b, attn_out_g, conv_out_g, w_out, xattn_norm_g, mem_norm_g, xattn_w_q, xattn_w_kv, xattn_w_o, ffn2_norm_g, ffn2_w_gate, ffn2_w_up, ffn2_w_down, final_norm_g):
    batch, seq, _ = x.shape
    depth = w_in.shape[0]
    n_win = seq // KEY_TILE
    row = lambda v: v.reshape(1, -1)

    mem_k, mem_v = _memkv(mem, mem_norm_g.reshape(depth, 1, D_MODEL), xattn_w_kv)
    fg = row(final_norm_g)
    tri = jnp.triu(jnp.ones((TOK_TILE, TOK_TILE), BF16))
    w_in_t = jnp.swapaxes(w_in, 1, 2)
    rep = jnp.asarray(_gate_replicator(), BF16)

    for l in range(depth):
        x2d = _ffn(x.reshape(batch * seq, D_MODEL), row(ffn1_norm_g[l]), ffn1_w_gate, ffn1_w_up,
                   ffn1_w_down, fg, l, final=False)
        x = x2d.reshape(batch, seq, D_MODEL)

        cw = jnp.broadcast_to(conv_w[l][:, None, :], (CONV_WIDTH, SUBLANES, D_CONV))
        q, k, v, c, u = _mix_in(
            x, row(mix_norm_g[l]), w_in_t, b_f[l].reshape(N_ATTN_HEADS, 1), tri, cw, row(conv_b[l]),
            row(conv_ln_g[l]), row(conv_ln_b[l]), row(conv_out_g[l]), l)

        crow = c.reshape(batch, N_ATTN_HEADS, n_win, KEY_TILE).transpose(0, 2, 1, 3)
        attn = _fox_attention(q, k, v, c, crow, rep, row(attn_out_g[l]))

        x = _mix_out(x, attn, u, w_out, row(xattn_norm_g[l]), xattn_w_q, mem_k, mem_v, xattn_w_o, l)

        last = l == depth - 1
        x2d = _ffn(x.reshape(batch * seq, D_MODEL), row(ffn2_norm_g[l]), ffn2_w_gate, ffn2_w_up,
                   ffn2_w_down, fg, l, final=last)
        x = x2d.reshape(batch, seq, D_MODEL)
    return x
```

```python
import functools
import math

import jax
import jax.numpy as jnp
import numpy as np
from jax import lax
from jax.experimental import pallas as pl
from jax.experimental.pallas import tpu as pltpu

F32 = jnp.float32
BF16 = jnp.bfloat16

D_MODEL = 1024
N_MEM = 256
D_ATTN = 512
D_CONV = 512
HEAD_DIM = 64
N_ATTN_HEADS = 8
CONV_WIDTH = 31
N_XATTN_HEADS = 4
XATTN_HEAD_DIM = 256
D_FF = 2816
D_IN = 3 * D_ATTN + N_ATTN_HEADS + 2 * D_CONV
EPS = 1e-6
NEG_INF = -1e30
LOG2E = math.log2(math.e)

LANES = 128
SUBLANES = 8
MXU_DIM = 256
VMEM_LIMIT = 56 * 1024 * 1024

TOK_TILE = 512
ATT_TILE = 512
KEY_TILE = ATT_TILE
DIAG_BAND = 256
FF_CHUNK = MXU_DIM
CONV_HALO = 32
CONV_ROWS = 64
OUT_TILE = 1024
MEM_BATCH = 4
N_PAIRS = D_ATTN // LANES


def _rms(x, g):
    return x * lax.rsqrt(jnp.mean(x * x, axis=-1, keepdims=True) + EPS) * g


def _const_spec(shape):
    nd = len(shape)
    return pl.BlockSpec(shape, lambda *_: (0,) * nd, pipeline_mode=pl.Buffered(1))


def _layer_spec(shape, layer):
    nd = len(shape)
    return pl.BlockSpec((1,) + shape, lambda *_: (layer,) + (0,) * nd, pipeline_mode=pl.Buffered(1))


def _ffn_kernel(x_ref, xn_ref, g_ref, wg_ref, wu_ref, wd_ref, fg_ref, o_ref,
                h_scr, hn_scr, act_scr, act0_scr, *, final):
    def hidden_chunk(h_ref, c):
        cols = slice(c * FF_CHUNK, (c + 1) * FF_CHUNK)
        h = h_ref[...]
        gate = jnp.dot(h, wg_ref[0, :, cols].astype(BF16), preferred_element_type=F32)
        up = jnp.dot(h, wu_ref[0, :, cols].astype(BF16), preferred_element_type=F32)
        return (gate * jax.nn.sigmoid(gate) * up).astype(BF16)

    @pl.when(pl.program_id(0) == 0)
    def _():
        hn_scr[...] = _rms(x_ref[...], g_ref[...]).astype(BF16)
        act0_scr[...] = hidden_chunk(hn_scr, 0)

    h_scr[...] = hn_scr[...]
    act_scr[:, :FF_CHUNK] = act0_scr[...]
    for c in range(1, D_FF // FF_CHUNK):
        act_scr[:, c * FF_CHUNK:(c + 1) * FF_CHUNK] = hidden_chunk(h_scr, c)
    hn_scr[...] = _rms(xn_ref[...], g_ref[...]).astype(BF16)
    y = jnp.dot(act_scr[...], wd_ref[0].astype(BF16), preferred_element_type=F32)
    act0_scr[...] = hidden_chunk(hn_scr, 0)
    out = x_ref[...] + 0.5 * y
    if final:
        out = _rms(out, fg_ref[...])
    o_ref[...] = out


def _ffn(x2d, g, wg, wu, wd, fg, layer, *, final):
    t = x2d.shape[0]
    n_tiles = t // TOK_TILE
    tile = pl.BlockSpec((TOK_TILE, D_MODEL), lambda i: (i, 0))
    next_tile = pl.BlockSpec((TOK_TILE, D_MODEL), lambda i: (jnp.minimum(i + 1, n_tiles - 1), 0))
    return pl.pallas_call(
        functools.partial(_ffn_kernel, final=final),
        out_shape=jax.ShapeDtypeStruct((t, D_MODEL), F32),
        grid=(n_tiles,),
        in_specs=[tile, next_tile, _const_spec((1, D_MODEL)), _layer_spec((D_MODEL, D_FF), layer),
                  _layer_spec((D_MODEL, D_FF), layer), _layer_spec((D_FF, D_MODEL), layer),
                  _const_spec((1, D_MODEL))],
        out_specs=tile,
        scratch_shapes=[pltpu.VMEM((TOK_TILE, D_MODEL), BF16), pltpu.VMEM((TOK_TILE, D_MODEL), BF16),
                        pltpu.VMEM((TOK_TILE, D_FF), BF16), pltpu.VMEM((TOK_TILE, FF_CHUNK), BF16)],
        compiler_params=pltpu.CompilerParams(
            dimension_semantics=("arbitrary",), vmem_limit_bytes=VMEM_LIMIT),
        name="ffn_final" if final else "ffn",
    )(x2d, x2d, g, wg, wu, wd, fg)


def _memkv_kernel(mem_ref, g_ref, w_ref, k_ref, v_ref):
    mem = mem_ref[...].reshape(MEM_BATCH * N_MEM, D_MODEL)
    mn = _rms(mem, g_ref[0]).astype(BF16)
    kv = jnp.dot(mn, w_ref[0].astype(BF16), preferred_element_type=F32)
    k_ref[0] = kv[:, :D_MODEL].astype(BF16).reshape(MEM_BATCH, N_MEM, D_MODEL)
    v_ref[0] = kv[:, D_MODEL:].astype(BF16).reshape(MEM_BATCH, N_MEM, D_MODEL)


def _memkv(mem, g, w_kv):
    depth, batch = g.shape[0], mem.shape[0]
    out = jax.ShapeDtypeStruct((depth, batch, N_MEM, D_MODEL), BF16)
    out_spec = pl.BlockSpec((1, MEM_BATCH, N_MEM, D_MODEL), lambda l, b: (l, b, 0, 0))
    return pl.pallas_call(
        _memkv_kernel,
        out_shape=(out, out),
        grid=(depth, batch // MEM_BATCH),
        in_specs=[pl.BlockSpec((MEM_BATCH, N_MEM, D_MODEL), lambda l, b: (b, 0, 0)),
                  pl.BlockSpec((1, 1, D_MODEL), lambda l, b: (l, 0, 0)),
                  pl.BlockSpec((1, D_MODEL, 2 * D_MODEL), lambda l, b: (l, 0, 0))],
        out_specs=(out_spec, out_spec),
        compiler_params=pltpu.CompilerParams(
            dimension_semantics=("arbitrary", "arbitrary"), vmem_limit_bytes=VMEM_LIMIT),
        name="mem_kv",
    )(mem, g, w_kv)


def _log_sigmoid(z):
    return jnp.minimum(z, 0.0) - jnp.log1p(jnp.exp(-jnp.abs(z)))


def _split3(a):
    hi = a.astype(BF16).astype(F32)
    r = a - hi
    mid = r.astype(BF16).astype(F32)
    lo = (r - mid).astype(BF16).astype(F32)
    return hi, mid, lo


def _mix_in_kernel(x_ref, g_ref, w_ref, bf_ref, tri_ref, cw_ref, cb_ref, lng_ref,
                   lnb_ref, cog_ref, q_ref, k_ref, v_ref, c_ref, u_ref,
                   wqkv_scr, wf_scr, wag_scr, h_scr, ubuf, shifted, carry):
    s_idx = pl.program_id(1)

    @pl.when(jnp.logical_and(pl.program_id(0) == 0, s_idx == 0))
    def _():
        f0 = 3 * D_ATTN
        wqkv_scr[...] = jnp.transpose(w_ref[0, :f0, :]).astype(BF16)
        wf_scr[...] = jnp.transpose(w_ref[0, f0:f0 + LANES, :]).astype(BF16)
        wag_scr[...] = jnp.transpose(w_ref[0, f0 + N_ATTN_HEADS:, :]).astype(BF16)

    def project(w_scr):
        return jnp.dot(h_scr[...], w_scr[...], preferred_element_type=F32)

    @pl.when(s_idx == 0)
    def _():
        carry[...] = jnp.zeros_like(carry)
        ubuf[0:CONV_HALO, :] = jnp.zeros((CONV_HALO, D_CONV), F32)

    h_scr[...] = _rms(x_ref[0], g_ref[...]).astype(BF16)

    ag = project(wag_scr)
    ubuf[CONV_HALO:, :] = ag[:, :D_CONV] * jax.nn.sigmoid(ag[:, D_CONV:])
    n_shift = TOK_TILE + CONV_HALO - SUBLANES
    for r in range(1, SUBLANES):
        shifted[r - 1] = ubuf[r:r + n_shift, :]

    qkv = project(wqkv_scr)
    q_ref[0] = (qkv[:, :D_ATTN] * (LOG2E / math.sqrt(HEAD_DIM))).astype(BF16)
    k_ref[0] = qkv[:, D_ATTN:2 * D_ATTN].astype(BF16)
    ones = jnp.ones((TOK_TILE, LANES), BF16)
    for p in range(N_PAIRS):
        v_ref[0, :, 2 * p * LANES:(2 * p + 1) * LANES] = (
            qkv[:, 2 * D_ATTN + p * LANES:2 * D_ATTN + (p + 1) * LANES].astype(BF16))
        v_ref[0, :, (2 * p + 1) * LANES:(2 * p + 2) * LANES] = ones

    f_t = jnp.transpose(project(wf_scr))
    logf = _log_sigmoid(f_t[:N_ATTN_HEADS, :] + bf_ref[...]) * LOG2E
    parts = jnp.concatenate(_split3(logf) + (jnp.zeros_like(logf),), axis=0).astype(BF16)
    sums = jnp.dot(parts, tri_ref[...], preferred_element_type=F32)
    h8 = N_ATTN_HEADS
    c_tile = carry[...] + sums[:h8] + sums[h8:2 * h8] + sums[2 * h8:3 * h8]
    c_ref[0] = c_tile
    carry[...] = c_tile[:, TOK_TILE - 1:TOK_TILE]

    for rb in range(TOK_TILE // CONV_ROWS):
        r0 = rb * CONV_ROWS
        acc = jnp.zeros((CONV_ROWS // SUBLANES, SUBLANES, D_CONV), F32)
        for tap in range(CONV_WIDTH):
            off = r0 + tap + CONV_HALO - (CONV_WIDTH - 1)
            sh, al = off % SUBLANES, (off // SUBLANES) * SUBLANES
            if sh == 0:
                blk = ubuf[al:al + CONV_ROWS, :]
            else:
                blk = shifted[sh - 1, al:al + CONV_ROWS, :]
            acc = acc + blk.reshape(CONV_ROWS // SUBLANES, SUBLANES, D_CONV) * cw_ref[tap]
        y = acc.reshape(CONV_ROWS, D_CONV) + cb_ref[...]
        mu = jnp.mean(y, axis=-1, keepdims=True)
        yc = y - mu
        yn = yc * lax.rsqrt(jnp.mean(yc * yc, axis=-1, keepdims=True) + EPS) * lng_ref[...] + lnb_ref[...]
        sw = yn * jax.nn.sigmoid(yn)
        u_ref[0, r0:r0 + CONV_ROWS, :] = _rms(sw, cog_ref[...]).astype(BF16)

    ubuf[0:CONV_HALO, :] = ubuf[TOK_TILE:TOK_TILE + CONV_HALO, :]


def _mix_in(x, g, w, bf, tri, cw, cb, lng, lnb, cog, layer):
    batch, seq, _ = x.shape
    tok = lambda w: pl.BlockSpec((1, TOK_TILE, w), lambda b, s: (b, s, 0))
    n_shift = TOK_TILE + CONV_HALO - SUBLANES
    outs = (jax.ShapeDtypeStruct((batch, seq, D_ATTN), BF16),) * 2 + (
        jax.ShapeDtypeStruct((batch, seq, 2 * D_ATTN), BF16),
        jax.ShapeDtypeStruct((batch, N_ATTN_HEADS, seq), F32),
        jax.ShapeDtypeStruct((batch, seq, D_CONV), BF16))
    return pl.pallas_call(
        _mix_in_kernel,
        out_shape=outs,
        grid=(batch, seq // TOK_TILE),
        in_specs=[tok(D_MODEL), _const_spec((1, D_MODEL)), _layer_spec((D_IN, D_MODEL), layer),
                  _const_spec((N_ATTN_HEADS, 1)), _const_spec((TOK_TILE, TOK_TILE)),
                  _const_spec((CONV_WIDTH, SUBLANES, D_CONV)),
                  _const_spec((1, D_CONV)), _const_spec((1, D_CONV)), _const_spec((1, D_CONV)),
                  _const_spec((1, D_CONV))],
        out_specs=(tok(D_ATTN), tok(D_ATTN), tok(2 * D_ATTN),
                   pl.BlockSpec((1, N_ATTN_HEADS, TOK_TILE), lambda b, s: (b, 0, s)), tok(D_CONV)),
        scratch_shapes=[pltpu.VMEM((D_MODEL, 3 * D_ATTN), BF16), pltpu.VMEM((D_MODEL, LANES), BF16),
                        pltpu.VMEM((D_MODEL, 2 * D_CONV), BF16),
                        pltpu.VMEM((TOK_TILE, D_MODEL), BF16),
                        pltpu.VMEM((TOK_TILE + CONV_HALO, D_CONV), F32),
                        pltpu.VMEM((SUBLANES - 1, n_shift, D_CONV), F32),
                        pltpu.VMEM((N_ATTN_HEADS, 1), F32)],
        compiler_params=pltpu.CompilerParams(
            dimension_semantics=("arbitrary", "arbitrary"), vmem_limit_bytes=VMEM_LIMIT),
        name="mix_in",
    )(x, g, w, bf, tri, cw, cb, lng, lnb, cog)


def _fox_kernel(q_ref, k_ref, v_ref, cq_ref, crow_ref, rep_ref, g_ref, o_ref,
                qm_scr, ct_scr, m_scr, acc_scr, s_scr):
    i = pl.program_id(1)
    n_full = i
    lane = lax.broadcasted_iota(jnp.int32, (ATT_TILE, LANES), 1)
    first = lane < HEAD_DIM

    for p in range(N_PAIRS):
        qp = q_ref[0, :, p * LANES:(p + 1) * LANES]
        zero = jnp.zeros_like(qp)
        qm_scr[2 * p] = jnp.where(first, qp, zero)
        qm_scr[2 * p + 1] = jnp.where(first, zero, qp)
    parts = jnp.concatenate(_split3(cq_ref[0]) + (jnp.zeros((N_ATTN_HEADS, ATT_TILE), F32),), axis=0)
    c_t = lax.dot_general(parts.astype(BF16), rep_ref[...], (((0,), (0,)), ((), ())),
                          preferred_element_type=F32)
    for h in range(N_ATTN_HEADS):
        ct_scr[h] = c_t[:, h * LANES:(h + 1) * LANES]

    def scores(t, h, n_chunk):
        k0 = pl.multiple_of(t * KEY_TILE, KEY_TILE)
        pair = slice((h // 2) * LANES, (h // 2 + 1) * LANES)
        c_s = crow_ref[0, t]
        s = lax.dot_general(qm_scr[h], k_ref[0, pl.ds(k0, n_chunk * LANES), pair],
                            (((1,), (1,)), ((), ())), preferred_element_type=F32)
        ct = ct_scr[h]
        for c in range(n_chunk):
            cols = slice(c * LANES, (c + 1) * LANES)
            s_scr[h, :, cols] = s[:, cols] + ct - c_s[h:h + 1, cols]

    def update(t, h, r0, n_rows, n_chunk, diagonal, first_window):
        k0 = pl.multiple_of(t * KEY_TILE, KEY_TILE)
        rows = slice(r0, r0 + n_rows)
        vcols = slice((h // 2) * 2 * LANES, (h // 2 + 1) * 2 * LANES)

        q_pos = lax.broadcasted_iota(jnp.int32, (n_rows, LANES), 0) + r0
        k_pos = lax.broadcasted_iota(jnp.int32, (n_rows, LANES), 1)

        def logits(c):
            sc = s_scr[h, rows, c * LANES:(c + 1) * LANES]
            if diagonal and (c + 1) * LANES > r0:
                sc = jnp.where(q_pos >= k_pos + c * LANES, sc, NEG_INF)
            return sc

        m_blk = functools.reduce(jnp.maximum, [logits(c) for c in range(n_chunk)])
        m_row = jnp.max(m_blk, axis=-1, keepdims=True)
        if first_window:
            m_new = jnp.broadcast_to(m_row, (n_rows, LANES))
        else:
            m_old = m_scr[h, rows, :]
            m_new = jnp.maximum(m_old, m_row)
            alpha = jnp.exp2(m_old - m_new)
            acc_scr[h, rows, :] = jnp.concatenate([alpha, alpha], axis=1) * acc_scr[h, rows, :]
        p_bf = jnp.concatenate([jnp.exp2(logits(c) - m_new) for c in range(n_chunk)], axis=1).astype(BF16)
        m_scr[h, rows, :] = m_new
        pv = jnp.dot(p_bf, v_ref[0, pl.ds(k0, n_chunk * LANES), vcols], preferred_element_type=F32)
        if first_window:
            acc_scr[h, rows, :] = pv
        else:
            acc_scr[h, rows, :] += pv

    all_chunks = KEY_TILE // LANES
    last = N_ATTN_HEADS - 1

    def full_window(t, first_window):
        for h in range(N_ATTN_HEADS):
            if h < last:
                scores(t, h + 1, all_chunks)
            else:
                scores(t + 1, 0, all_chunks)
            update(t, h, 0, ATT_TILE, all_chunks, False, first_window)

    def diagonal_window(first_window):
        for h in range(N_ATTN_HEADS):
            if h < last:
                scores(n_full, h + 1, all_chunks)
            for r0 in range(0, ATT_TILE, DIAG_BAND):
                update(n_full, h, r0, DIAG_BAND, (r0 + DIAG_BAND) // LANES, True, first_window)

    scores(0, 0, all_chunks)

    @pl.when(i == 0)
    def _():
        diagonal_window(True)

    @pl.when(i > 0)
    def _():
        full_window(0, True)
        lax.fori_loop(1, n_full, lambda t, c: (full_window(t, False), c)[1], 0)
        diagonal_window(False)

    outs = []
    for p in range(N_PAIRS):
        o0 = acc_scr[2 * p, :, :LANES] / acc_scr[2 * p, :, LANES:]
        o1 = acc_scr[2 * p + 1, :, :LANES] / acc_scr[2 * p + 1, :, LANES:]
        outs.append(jnp.where(first, o0, o1))
    o_ref[0] = _rms(jnp.concatenate(outs, axis=1), g_ref[...]).astype(BF16)


def _fox_attention(q, k, v, c, crow, rep, g):
    batch, seq, _ = q.shape
    n_win = seq // KEY_TILE
    stat = pltpu.VMEM((N_ATTN_HEADS, ATT_TILE, LANES), F32)
    return pl.pallas_call(
        _fox_kernel,
        out_shape=jax.ShapeDtypeStruct((batch, seq, D_ATTN), BF16),
        grid=(batch, seq // ATT_TILE),
        in_specs=[pl.BlockSpec((1, ATT_TILE, D_ATTN), lambda b, i: (b, i, 0)),
                  pl.BlockSpec((1, seq, D_ATTN), lambda b, i: (b, 0, 0)),
                  pl.BlockSpec((1, seq, 2 * D_ATTN), lambda b, i: (b, 0, 0)),
                  pl.BlockSpec((1, N_ATTN_HEADS, ATT_TILE), lambda b, i: (b, 0, i)),
                  pl.BlockSpec((1, n_win, N_ATTN_HEADS, KEY_TILE), lambda b, i: (b, 0, 0, 0)),
                  _const_spec((4 * N_ATTN_HEADS, N_ATTN_HEADS * LANES)), _const_spec((1, D_ATTN))],
        out_specs=pl.BlockSpec((1, ATT_TILE, D_ATTN), lambda b, i: (b, i, 0)),
        scratch_shapes=[pltpu.VMEM((N_ATTN_HEADS, ATT_TILE, LANES), BF16), stat, stat,
                        pltpu.VMEM((N_ATTN_HEADS, ATT_TILE, 2 * LANES), F32),
                        pltpu.VMEM((N_ATTN_HEADS, ATT_TILE, KEY_TILE), F32)],
        compiler_params=pltpu.CompilerParams(
            dimension_semantics=("arbitrary", "arbitrary"), vmem_limit_bytes=VMEM_LIMIT),
        name="fox_attn",
    )(q, k, v, c, crow, rep, g)


def _mix_out_kernel(x_ref, a_ref, u_ref, wout_ref, xg_ref, wq_ref, mk_ref, mv_ref, wo_ref,
                    o_ref, o_scr):
    y = jnp.dot(a_ref[0], wout_ref[0, :D_ATTN, :].astype(BF16), preferred_element_type=F32)
    y = y + jnp.dot(u_ref[0], wout_ref[0, D_ATTN:, :].astype(BF16), preferred_element_type=F32)
    x = x_ref[0] + y

    hq = _rms(x, xg_ref[...]).astype(BF16)
    q = jnp.dot(hq, wq_ref[0].astype(BF16), preferred_element_type=F32) * (LOG2E / math.sqrt(XATTN_HEAD_DIM))
    q = q.astype(BF16)
    for h in range(N_XATTN_HEADS):
        cols = slice(h * XATTN_HEAD_DIM, (h + 1) * XATTN_HEAD_DIM)
        s = lax.dot_general(q[:, cols], mk_ref[0, 0, :, cols], (((1,), (1,)), ((), ())),
                            preferred_element_type=F32)
        p = jnp.exp2(s - jnp.max(s, axis=-1, keepdims=True))
        pv = jnp.dot(p.astype(BF16), mv_ref[0, 0, :, cols], preferred_element_type=F32)
        o_scr[:, cols] = (pv / jnp.sum(p, axis=-1, keepdims=True)).astype(BF16)
    o_ref[0] = x + jnp.dot(o_scr[...], wo_ref[0].astype(BF16), preferred_element_type=F32)


def _mix_out(x, attn, u, wout, xg, wq, mk, mv, wo, layer):
    batch, seq, _ = x.shape
    tok = lambda w: pl.BlockSpec((1, OUT_TILE, w), lambda b, s: (b, s, 0))
    mem_spec = pl.BlockSpec((1, 1, N_MEM, D_MODEL), lambda b, s: (layer, b, 0, 0))
    return pl.pallas_call(
        _mix_out_kernel,
        out_shape=jax.ShapeDtypeStruct((batch, seq, D_MODEL), F32),
        grid=(batch, seq // OUT_TILE),
        in_specs=[tok(D_MODEL), tok(D_ATTN), tok(D_CONV),
                  _layer_spec((D_MODEL, D_MODEL), layer), _const_spec((1, D_MODEL)),
                  _layer_spec((D_MODEL, D_MODEL), layer), mem_spec, mem_spec,
                  _layer_spec((D_MODEL, D_MODEL), layer)],
        out_specs=tok(D_MODEL),
        scratch_shapes=[pltpu.VMEM((OUT_TILE, D_MODEL), BF16)],
        compiler_params=pltpu.CompilerParams(
            dimension_semantics=("arbitrary", "arbitrary"), vmem_limit_bytes=VMEM_LIMIT),
        name="mix_out",
    )(x, attn, u, wout, xg, wq, mk, mv, wo)


def _gate_replicator():
    mat = np.zeros((4 * N_ATTN_HEADS, N_ATTN_HEADS * LANES), np.float32)
    for h in range(N_ATTN_HEADS):
        for part in range(3):
            mat[part * N_ATTN_HEADS + h, h * LANES:(h + 1) * LANES] = 1.0
    return mat


def kernel(x, mem, ffn1_norm_g, ffn1_w_gate, ffn1_w_up, ffn1_w_down, mix_norm_g, w_in, b_f, conv_w, conv_b, conv_ln_g, conv_ln_b, attn_out_g, conv_out_g, w_out, xattn_norm_g, mem_norm_g, xattn_w_q, xattn_w_kv, xattn_w_o, ffn2_norm_g, ffn2_w_gate, ffn2_w_up, ffn2_w_down, final_norm_g):
    batch, seq, _ = x.shape
    depth = w_in.shape[0]
    n_win = seq // KEY_TILE
    row = lambda v: v.reshape(1, -1)

    mem_k, mem_v = _memkv(mem, mem_norm_g.reshape(depth, 1, D_MODEL), xattn_w_kv)
    fg = row(final_norm_g)
    tri = jnp.triu(jnp.ones((TOK_TILE, TOK_TILE), BF16))
    w_in_t = jnp.swapaxes(w_in, 1, 2)
    rep = jnp.asarray(_gate_replicator(), BF16)

    for l in range(depth):
        x2d = _ffn(x.reshape(batch * seq, D_MODEL), row(ffn1_norm_g[l]), ffn1_w_gate, ffn1_w_up,
                   ffn1_w_down, fg, l, final=False)
        x = x2d.reshape(batch, seq, D_MODEL)

        cw = jnp.broadcast_to(conv_w[l][:, None, :], (CONV_WIDTH, SUBLANES, D_CONV))
        q, k, v, c, u = _mix_in(
            x, row(mix_norm_g[l]), w_in_t, b_f[l].reshape(N_ATTN_HEADS, 1), tri, cw, row(conv_b[l]),
            row(conv_ln_g[l]), row(conv_ln_b[l]), row(conv_out_g[l]), l)

        crow = c.reshape(batch, N_ATTN_HEADS, n_win, KEY_TILE).transpose(0, 2, 1, 3)
        attn = _fox_attention(q, k, v, c, crow, rep, row(attn_out_g[l]))

        x = _mix_out(x, attn, u, w_out, row(xattn_norm_g[l]), xattn_w_q, mem_k, mem_v, xattn_w_o, l)

        last = l == depth - 1
        x2d = _ffn(x.reshape(batch * seq, D_MODEL), row(ffn2_norm_g[l]), ffn2_w_gate, ffn2_w_up,
                   ffn2_w_down, fg, l, final=last)
        x = x2d.reshape(batch, seq, D_MODEL)
    return x
```

```python
import functools
import math

import jax
import jax.numpy as jnp
import numpy as np
from jax import lax
from jax.experimental import pallas as pl
from jax.experimental.pallas import tpu as pltpu

F32 = jnp.float32
BF16 = jnp.bfloat16

D_MODEL = 1024
N_MEM = 256
D_ATTN = 512
D_CONV = 512
HEAD_DIM = 64
N_ATTN_HEADS = 8
CONV_WIDTH = 31
N_XATTN_HEADS = 4
XATTN_HEAD_DIM = 256
D_FF = 2816
D_IN = 3 * D_ATTN + N_ATTN_HEADS + 2 * D_CONV
EPS = 1e-6
NEG_INF = -1e30
LOG2E = math.log2(math.e)

LANES = 128
SUBLANES = 8
MXU_DIM = 256
VMEM_LIMIT = 56 * 1024 * 1024

TOK_TILE = 512
ATT_TILE = 512
KEY_TILE = ATT_TILE
DIAG_BAND = 256
FF_CHUNK = MXU_DIM
CONV_HALO = 32
CONV_ROWS = 64
OUT_TILE = 1024
MEM_BATCH = 4
N_PAIRS = D_ATTN // LANES


def _rms(x, g):
    return x * lax.rsqrt(jnp.mean(x * x, axis=-1, keepdims=True) + EPS) * g


def _const_spec(shape):
    nd = len(shape)
    return pl.BlockSpec(shape, lambda *_: (0,) * nd, pipeline_mode=pl.Buffered(1))


def _layer_spec(shape, layer):
    nd = len(shape)
    return pl.BlockSpec((1,) + shape, lambda *_: (layer,) + (0,) * nd, pipeline_mode=pl.Buffered(1))


def _ffn_kernel(x_ref, xn_ref, g_ref, wg_ref, wu_ref, wd_ref, fg_ref, o_ref,
                h_scr, hn_scr, act_scr, act0_scr, *, final):
    def hidden_chunk(h_ref, c):
        cols = slice(c * FF_CHUNK, (c + 1) * FF_CHUNK)
        h = h_ref[...]
        gate = jnp.dot(h, wg_ref[0, :, cols].astype(BF16), preferred_element_type=F32)
        up = jnp.dot(h, wu_ref[0, :, cols].astype(BF16), preferred_element_type=F32)
        return (gate * jax.nn.sigmoid(gate) * up).astype(BF16)

    @pl.when(pl.program_id(0) == 0)
    def _():
        hn_scr[...] = _rms(x_ref[...], g_ref[...]).astype(BF16)
        act0_scr[...] = hidden_chunk(hn_scr, 0)

    h_scr[...] = hn_scr[...]
    act_scr[:, :FF_CHUNK] = act0_scr[...]
    for c in range(1, D_FF // FF_CHUNK):
        act_scr[:, c * FF_CHUNK:(c + 1) * FF_CHUNK] = hidden_chunk(h_scr, c)
    hn_scr[...] = _rms(xn_ref[...], g_ref[...]).astype(BF16)
    y = jnp.dot(act_scr[...], wd_ref[0].astype(BF16), preferred_element_type=F32)
    act0_scr[...] = hidden_chunk(hn_scr, 0)
    out = x_ref[...] + 0.5 * y
    if final:
        out = _rms(out, fg_ref[...])
    o_ref[...] = out


def _ffn(x2d, g, wg, wu, wd, fg, layer, *, final):
    t = x2d.shape[0]
    n_tiles = t // TOK_TILE
    tile = pl.BlockSpec((TOK_TILE, D_MODEL), lambda i: (i, 0))
    next_tile = pl.BlockSpec((TOK_TILE, D_MODEL), lambda i: (jnp.minimum(i + 1, n_tiles - 1), 0))
    return pl.pallas_call(
        functools.partial(_ffn_kernel, final=final),
        out_shape=jax.ShapeDtypeStruct((t, D_MODEL), F32),
        grid=(n_tiles,),
        in_specs=[tile, next_tile, _const_spec((1, D_MODEL)), _layer_spec((D_MODEL, D_FF), layer),
                  _layer_spec((D_MODEL, D_FF), layer), _layer_spec((D_FF, D_MODEL), layer),
                  _const_spec((1, D_MODEL))],
        out_specs=tile,
        scratch_shapes=[pltpu.VMEM((TOK_TILE, D_MODEL), BF16), pltpu.VMEM((TOK_TILE, D_MODEL), BF16),
                        pltpu.VMEM((TOK_TILE, D_FF), BF16), pltpu.VMEM((TOK_TILE, FF_CHUNK), BF16)],
        compiler_params=pltpu.CompilerParams(
            dimension_semantics=("arbitrary",), vmem_limit_bytes=VMEM_LIMIT),
        name="ffn_final" if final else "ffn",
    )(x2d, x2d, g, wg, wu, wd, fg)


def _memkv_kernel(mem_ref, g_ref, w_ref, k_ref, v_ref):
    mem = mem_ref[...].reshape(MEM_BATCH * N_MEM, D_MODEL)
    mn = _rms(mem, g_ref[0]).astype(BF16)
    kv = jnp.dot(mn, w_ref[0].astype(BF16), preferred_element_type=F32)
    k_ref[0] = kv[:, :D_MODEL].astype(BF16).reshape(MEM_BATCH, N_MEM, D_MODEL)
    v_ref[0] = kv[:, D_MODEL:].astype(BF16).reshape(MEM_BATCH, N_MEM, D_MODEL)


def _memkv(mem, g, w_kv):
    depth, batch = g.shape[0], mem.shape[0]
    out = jax.ShapeDtypeStruct((depth, batch, N_MEM, D_MODEL), BF16)
    out_spec = pl.BlockSpec((1, MEM_BATCH, N_MEM, D_MODEL), lambda l, b: (l, b, 0, 0))
    return pl.pallas_call(
        _memkv_kernel,
        out_shape=(out, out),
        grid=(depth, batch // MEM_BATCH),
        in_specs=[pl.BlockSpec((MEM_BATCH, N_MEM, D_MODEL), lambda l, b: (b, 0, 0)),
                  pl.BlockSpec((1, 1, D_MODEL), lambda l, b: (l, 0, 0)),
                  pl.BlockSpec((1, D_MODEL, 2 * D_MODEL), lambda l, b: (l, 0, 0))],
        out_specs=(out_spec, out_spec),
        compiler_params=pltpu.CompilerParams(
            dimension_semantics=("arbitrary", "arbitrary"), vmem_limit_bytes=VMEM_LIMIT),
        name="mem_kv",
    )(mem, g, w_kv)


def _log_sigmoid(z):
    return jnp.minimum(z, 0.0) - jnp.log1p(jnp.exp(-jnp.abs(z)))


def _split3(a):
    hi = a.astype(BF16).astype(F32)
    r = a - hi
    mid = r.astype(BF16).astype(F32)
    lo = (r - mid).astype(BF16).astype(F32)
    return hi, mid, lo


def _mix_in_kernel(x_ref, g_ref, w_ref, bf_ref, tri_ref, cw_ref, cb_ref, lng_ref,
                   lnb_ref, cog_ref, q_ref, k_ref, v_ref, c_ref, u_ref,
                   wqkv_scr, wf_scr, wag_scr, h_scr, ubuf, shifted, carry):
    s_idx = pl.program_id(1)

    @pl.when(jnp.logical_and(pl.program_id(0) == 0, s_idx == 0))
    def _():
        f0 = 3 * D_ATTN
        wqkv_scr[...] = jnp.transpose(w_ref[0, :f0, :]).astype(BF16)
        wf_scr[...] = jnp.transpose(w_ref[0, f0:f0 + LANES, :]).astype(BF16)
        wag_scr[...] = jnp.transpose(w_ref[0, f0 + N_ATTN_HEADS:, :]).astype(BF16)

    def project(w_scr):
        return jnp.dot(h_scr[...], w_scr[...], preferred_element_type=F32)

    @pl.when(s_idx == 0)
    def _():
        carry[...] = jnp.zeros_like(carry)
        ubuf[0:CONV_HALO, :] = jnp.zeros((CONV_HALO, D_CONV), F32)

    h_scr[...] = _rms(x_ref[0], g_ref[...]).astype(BF16)

    ag = project(wag_scr)
    ubuf[CONV_HALO:, :] = ag[:, :D_CONV] * jax.nn.sigmoid(ag[:, D_CONV:])
    n_shift = TOK_TILE + CONV_HALO - SUBLANES
    for r in range(1, SUBLANES):
        shifted[r - 1] = ubuf[r:r + n_shift, :]

    qkv = project(wqkv_scr)
    q_ref[0] = (qkv[:, :D_ATTN] * (LOG2E / math.sqrt(HEAD_DIM))).astype(BF16)
    k_ref[0] = qkv[:, D_ATTN:2 * D_ATTN].astype(BF16)
    ones = jnp.ones((TOK_TILE, LANES), BF16)
    for p in range(N_PAIRS):
        v_ref[0, :, 2 * p * LANES:(2 * p + 1) * LANES] = (
            qkv[:, 2 * D_ATTN + p * LANES:2 * D_ATTN + (p + 1) * LANES].astype(BF16))
        v_ref[0, :, (2 * p + 1) * LANES:(2 * p + 2) * LANES] = ones

    f_t = jnp.transpose(project(wf_scr))
    logf = _log_sigmoid(f_t[:N_ATTN_HEADS, :] + bf_ref[...]) * LOG2E
    parts = jnp.concatenate(_split3(logf) + (jnp.zeros_like(logf),), axis=0).astype(BF16)
    sums = jnp.dot(parts, tri_ref[...], preferred_element_type=F32)
    h8 = N_ATTN_HEADS
    c_tile = carry[...] + sums[:h8] + sums[h8:2 * h8] + sums[2 * h8:3 * h8]
    c_ref[0] = c_tile
    carry[...] = c_tile[:, TOK_TILE - 1:TOK_TILE]

    for rb in range(TOK_TILE // CONV_ROWS):
        r0 = rb * CONV_ROWS
        acc = jnp.zeros((CONV_ROWS // SUBLANES, SUBLANES, D_CONV), F32)
        for tap in range(CONV_WIDTH):
            off = r0 + tap + CONV_HALO - (CONV_WIDTH - 1)
            sh, al = off % SUBLANES, (off // SUBLANES) * SUBLANES
            if sh == 0:
                blk = ubuf[al:al + CONV_ROWS, :]
            else:
                blk = shifted[sh - 1, al:al + CONV_ROWS, :]
            acc = acc + blk.reshape(CONV_ROWS // SUBLANES, SUBLANES, D_CONV) * cw_ref[tap]
        y = acc.reshape(CONV_ROWS, D_CONV) + cb_ref[...]
        mu = jnp.mean(y, axis=-1, keepdims=True)
        yc = y - mu
        yn = yc * lax.rsqrt(jnp.mean(yc * yc, axis=-1, keepdims=True) + EPS) * lng_ref[...] + lnb_ref[...]
        sw = yn * jax.nn.sigmoid(yn)
        u_ref[0, r0:r0 + CONV_ROWS, :] = _rms(sw, cog_ref[...]).astype(BF16)

    ubuf[0:CONV_HALO, :] = ubuf[TOK_TILE:TOK_TILE + CONV_HALO, :]


def _mix_in(x, g, w, bf, tri, cw, cb, lng, lnb, cog, layer):
    batch, seq, _ = x.shape
    tok = lambda w: pl.BlockSpec((1, TOK_TILE, w), lambda b, s: (b, s, 0))
    n_shift = TOK_TILE + CONV_HALO - SUBLANES
    outs = (jax.ShapeDtypeStruct((batch, seq, D_ATTN), BF16),) * 2 + (
        jax.ShapeDtypeStruct((batch, seq, 2 * D_ATTN), BF16),
        jax.ShapeDtypeStruct((batch, N_ATTN_HEADS, seq), F32),
        jax.ShapeDtypeStruct((batch, seq, D_CONV), BF16))
    return pl.pallas_call(
        _mix_in_kernel,
        out_shape=outs,
        grid=(batch, seq // TOK_TILE),
        in_specs=[tok(D_MODEL), _const_spec((1, D_MODEL)), _layer_spec((D_IN, D_MODEL), layer),
                  _const_spec((N_ATTN_HEADS, 1)), _const_spec((TOK_TILE, TOK_TILE)),
                  _const_spec((CONV_WIDTH, SUBLANES, D_CONV)),
                  _const_spec((1, D_CONV)), _const_spec((1, D_CONV)), _const_spec((1, D_CONV)),
                  _const_spec((1, D_CONV))],
        out_specs=(tok(D_ATTN), tok(D_ATTN), tok(2 * D_ATTN),
                   pl.BlockSpec((1, N_ATTN_HEADS, TOK_TILE), lambda b, s: (b, 0, s)), tok(D_CONV)),
        scratch_shapes=[pltpu.VMEM((D_MODEL, 3 * D_ATTN), BF16), pltpu.VMEM((D_MODEL, LANES), BF16),
                        pltpu.VMEM((D_MODEL, 2 * D_CONV), BF16),
                        pltpu.VMEM((TOK_TILE, D_MODEL), BF16),
                        pltpu.VMEM((TOK_TILE + CONV_HALO, D_CONV), F32),
                        pltpu.VMEM((SUBLANES - 1, n_shift, D_CONV), F32),
                        pltpu.VMEM((N_ATTN_HEADS, 1), F32)],
        compiler_params=pltpu.CompilerParams(
            dimension_semantics=("arbitrary", "arbitrary"), vmem_limit_bytes=VMEM_LIMIT),
        name="mix_in",
    )(x, g, w, bf, tri, cw, cb, lng, lnb, cog)


def _fox_kernel(q_ref, k_ref, v_ref, cq_ref, crow_ref, rep_ref, g_ref, o_ref,
                qm_scr, ct_scr, m_scr, acc_scr, s_scr):
    i = pl.program_id(1)
    n_full = i
    lane = lax.broadcasted_iota(jnp.int32, (ATT_TILE, LANES), 1)
    first = lane < HEAD_DIM

    for p in range(N_PAIRS):
        qp = q_ref[0, :, p * LANES:(p + 1) * LANES]
        zero = jnp.zeros_like(qp)
        qm_scr[2 * p] = jnp.where(first, qp, zero)
        qm_scr[2 * p + 1] = jnp.where(first, zero, qp)
    parts = jnp.concatenate(_split3(cq_ref[0]) + (jnp.zeros((N_ATTN_HEADS, ATT_TILE), F32),), axis=0)
    c_t = lax.dot_general(parts.astype(BF16), rep_ref[...], (((0,), (0,)), ((), ())),
                          preferred_element_type=F32)
    for h in range(N_ATTN_HEADS):
        ct_scr[h] = c_t[:, h * LANES:(h + 1) * LANES]

    def scores(t, h, n_chunk):
        k0 = pl.multiple_of(t * KEY_TILE, KEY_TILE)
        pair = slice((h // 2) * LANES, (h // 2 + 1) * LANES)
        c_s = crow_ref[0, t]
        s = lax.dot_general(qm_scr[h], k_ref[0, pl.ds(k0, n_chunk * LANES), pair],
                            (((1,), (1,)), ((), ())), preferred_element_type=F32)
        ct = ct_scr[h]
        for c in range(n_chunk):
            cols = slice(c * LANES, (c + 1) * LANES)
            s_scr[h, :, cols] = s[:, cols] + ct - c_s[h:h + 1, cols]

    def update(t, h, r0, n_rows, n_chunk, diagonal, first_window):
        k0 = pl.multiple_of(t * KEY_TILE, KEY_TILE)
        rows = slice(r0, r0 + n_rows)
        vcols = slice((h // 2) * 2 * LANES, (h // 2 + 1) * 2 * LANES)

        q_pos = lax.broadcasted_iota(jnp.int32, (n_rows, LANES), 0) + r0
        k_pos = lax.broadcasted_iota(jnp.int32, (n_rows, LANES), 1)

        def logits(c):
            sc = s_scr[h, rows, c * LANES:(c + 1) * LANES]
            if diagonal and (c + 1) * LANES > r0:
                sc = jnp.where(q_pos >= k_pos + c * LANES, sc, NEG_INF)
            return sc

        m_blk = functools.reduce(jnp.maximum, [logits(c) for c in range(n_chunk)])
        m_row = jnp.max(m_blk, axis=-1, keepdims=True)
        if first_window:
            m_new = jnp.broadcast_to(m_row, (n_rows, LANES))
        else:
            m_old = m_scr[h, rows, :]
            m_new = jnp.maximum(m_old, m_row)
            alpha = jnp.exp2(m_old - m_new)
            acc_scr[h, rows, :] = jnp.concatenate([alpha, alpha], axis=1) * acc_scr[h, rows, :]
        p_bf = jnp.concatenate([jnp.exp2(logits(c) - m_new) for c in range(n_chunk)], axis=1).astype(BF16)
        m_scr[h, rows, :] = m_new
        pv = jnp.dot(p_bf, v_ref[0, pl.ds(k0, n_chunk * LANES), vcols], preferred_element_type=F32)
        if first_window:
            acc_scr[h, rows, :] = pv
        else:
            acc_scr[h, rows, :] += pv

    all_chunks = KEY_TILE // LANES
    last = N_ATTN_HEADS - 1

    def full_window(t, first_window):
        for h in range(N_ATTN_HEADS):
            if h < last:
                scores(t, h + 1, all_chunks)
            else:
                scores(t + 1, 0, all_chunks)
            update(t, h, 0, ATT_TILE, all_chunks, False, first_window)

    def diagonal_window(first_window):
        for h in range(N_ATTN_HEADS):
            if h < last:
                scores(n_full, h + 1, all_chunks)
            for r0 in range(0, ATT_TILE, DIAG_BAND):
                update(n_full, h, r0, DIAG_BAND, (r0 + DIAG_BAND) // LANES, True, first_window)

    scores(0, 0, all_chunks)

    @pl.when(i == 0)
    def _():
        diagonal_window(True)

    @pl.when(i > 0)
    def _():
        full_window(0, True)
        lax.fori_loop(1, n_full, lambda t, c: (full_window(t, False), c)[1], 0)
        diagonal_window(False)

    outs = []
    for p in range(N_PAIRS):
        o0 = acc_scr[2 * p, :, :LANES] / acc_scr[2 * p, :, LANES:]
        o1 = acc_scr[2 * p + 1, :, :LANES] / acc_scr[2 * p + 1, :, LANES:]
        outs.append(jnp.where(first, o0, o1))
    o_ref[0] = _rms(jnp.concatenate(outs, axis=1), g_ref[...]).astype(BF16)


def _fox_attention(q, k, v, c, crow, rep, g):
    batch, seq, _ = q.shape
    n_win = seq // KEY_TILE
    stat = pltpu.VMEM((N_ATTN_HEADS, ATT_TILE, LANES), F32)
    return pl.pallas_call(
        _fox_kernel,
        out_shape=jax.ShapeDtypeStruct((batch, seq, D_ATTN), BF16),
        grid=(batch, seq // ATT_TILE),
        in_specs=[pl.BlockSpec((1, ATT_TILE, D_ATTN), lambda b, i: (b, i, 0)),
                  pl.BlockSpec((1, seq, D_ATTN), lambda b, i: (b, 0, 0)),
                  pl.BlockSpec((1, seq, 2 * D_ATTN), lambda b, i: (b, 0, 0)),
                  pl.BlockSpec((1, N_ATTN_HEADS, ATT_TILE), lambda b, i: (b, 0, i)),
                  pl.BlockSpec((1, n_win, N_ATTN_HEADS, KEY_TILE), lambda b, i: (b, 0, 0, 0)),
                  _const_spec((4 * N_ATTN_HEADS, N_ATTN_HEADS * LANES)), _const_spec((1, D_ATTN))],
        out_specs=pl.BlockSpec((1, ATT_TILE, D_ATTN), lambda b, i: (b, i, 0)),
        scratch_shapes=[pltpu.VMEM((N_ATTN_HEADS, ATT_TILE, LANES), BF16), stat, stat,
                        pltpu.VMEM((N_ATTN_HEADS, ATT_TILE, 2 * LANES), F32),
                        pltpu.VMEM((N_ATTN_HEADS, ATT_TILE, KEY_TILE), F32)],
        compiler_params=pltpu.CompilerParams(
            dimension_semantics=("arbitrary", "arbitrary"), vmem_limit_bytes=VMEM_LIMIT),
        name="fox_attn",
    )(q, k, v, c, crow, rep, g)


def _mix_out_kernel(x_ref, a_ref, u_ref, wout_ref, xg_ref, wq_ref, mk_ref, mv_ref, wo_ref,
                    o_ref, o_scr):
    mixed = jnp.concatenate([a_ref[0], u_ref[0]], axis=1)
    x = x_ref[0] + jnp.dot(mixed, wout_ref[0].astype(BF16), preferred_element_type=F32)

    hq = _rms(x, xg_ref[...]).astype(BF16)
    q = jnp.dot(hq, wq_ref[0].astype(BF16), preferred_element_type=F32) * (LOG2E / math.sqrt(XATTN_HEAD_DIM))
    q = q.astype(BF16)
    for h in range(N_XATTN_HEADS):
        cols = slice(h * XATTN_HEAD_DIM, (h + 1) * XATTN_HEAD_DIM)
        s = lax.dot_general(q[:, cols], mk_ref[0, 0, :, cols], (((1,), (1,)), ((), ())),
                            preferred_element_type=F32)
        p = jnp.exp2(s - jnp.max(s, axis=-1, keepdims=True))
        pv = jnp.dot(p.astype(BF16), mv_ref[0, 0, :, cols], preferred_element_type=F32)
        o_scr[:, cols] = (pv / jnp.sum(p, axis=-1, keepdims=True)).astype(BF16)
    o_ref[0] = x + jnp.dot(o_scr[...], wo_ref[0].astype(BF16), preferred_element_type=F32)


def _mix_out(x, attn, u, wout, xg, wq, mk, mv, wo, layer):
    batch, seq, _ = x.shape
    tok = lambda w: pl.BlockSpec((1, OUT_TILE, w), lambda b, s: (b, s, 0))
    mem_spec = pl.BlockSpec((1, 1, N_MEM, D_MODEL), lambda b, s: (layer, b, 0, 0))
    return pl.pallas_call(
        _mix_out_kernel,
        out_shape=jax.ShapeDtypeStruct((batch, seq, D_MODEL), F32),
        grid=(batch, seq // OUT_TILE),
        in_specs=[tok(D_MODEL), tok(D_ATTN), tok(D_CONV),
                  _layer_spec((D_MODEL, D_MODEL), layer), _const_spec((1, D_MODEL)),
                  _layer_spec((D_MODEL, D_MODEL), layer), mem_spec, mem_spec,
                  _layer_spec((D_MODEL, D_MODEL), layer)],
        out_specs=tok(D_MODEL),
        scratch_shapes=[pltpu.VMEM((OUT_TILE, D_MODEL), BF16)],
        compiler_params=pltpu.CompilerParams(
            dimension_semantics=("arbitrary", "arbitrary"), vmem_limit_bytes=VMEM_LIMIT),
        name="mix_out",
    )(x, attn, u, wout, xg, wq, mk, mv, wo)


def _gate_replicator():
    mat = np.zeros((4 * N_ATTN_HEADS, N_ATTN_HEADS * LANES), np.float32)
    for h in range(N_ATTN_HEADS):
        for part in range(3):
            mat[part * N_ATTN_HEADS + h, h * LANES:(h + 1) * LANES] = 1.0
    return mat


def kernel(x, mem, ffn1_norm_g, ffn1_w_gate, ffn1_w_up, ffn1_w_down, mix_norm_g, w_in, b_f, conv_w, conv_b, conv_ln_g, conv_ln_b, attn_out_g, conv_out_g, w_out, xattn_norm_g, mem_norm_g, xattn_w_q, xattn_w_kv, xattn_w_o, ffn2_norm_g, ffn2_w_gate, ffn2_w_up, ffn2_w_down, final_norm_g):
    batch, seq, _ = x.shape
    depth = w_in.shape[0]
    n_win = seq // KEY_TILE
    row = lambda v: v.reshape(1, -1)

    mem_k, mem_v = _memkv(mem, mem_norm_g.reshape(depth, 1, D_MODEL), xattn_w_kv)
    fg = row(final_norm_g)
    tri = jnp.triu(jnp.ones((TOK_TILE, TOK_TILE), BF16))
    w_in_t = jnp.swapaxes(w_in, 1, 2)
    rep = jnp.asarray(_gate_replicator(), BF16)

    for l in range(depth):
        x2d = _ffn(x.reshape(batch * seq, D_MODEL), row(ffn1_norm_g[l]), ffn1_w_gate, ffn1_w_up,
                   ffn1_w_down, fg, l, final=False)
        x = x2d.reshape(batch, seq, D_MODEL)

        cw = jnp.broadcast_to(conv_w[l][:, None, :], (CONV_WIDTH, SUBLANES, D_CONV))
        q, k, v, c, u = _mix_in(
            x, row(mix_norm_g[l]), w_in_t, b_f[l].reshape(N_ATTN_HEADS, 1), tri, cw, row(conv_b[l]),
            row(conv_ln_g[l]), row(conv_ln_b[l]), row(conv_out_g[l]), l)

        crow = c.reshape(batch, N_ATTN_HEADS, n_win, KEY_TILE).transpose(0, 2, 1, 3)
        attn = _fox_attention(q, k, v, c, crow, rep, row(attn_out_g[l]))

        x = _mix_out(x, attn, u, w_out, row(xattn_norm_g[l]), xattn_w_q, mem_k, mem_v, xattn_w_o, l)

        last = l == depth - 1
        x2d = _ffn(x.reshape(batch * seq, D_MODEL), row(ffn2_norm_g[l]), ffn2_w_gate, ffn2_w_up,
                   ffn2_w_down, fg, l, final=last)
        x = x2d.reshape(batch, seq, D_MODEL)
    return x
```

```python
import functools
import math

import jax
import jax.numpy as jnp
import numpy as np
from jax import lax
from jax.experimental import pallas as pl
from jax.experimental.pallas import tpu as pltpu

F32 = jnp.float32
BF16 = jnp.bfloat16

D_MODEL = 1024
N_MEM = 256
D_ATTN = 512
D_CONV = 512
HEAD_DIM = 64
N_ATTN_HEADS = 8
CONV_WIDTH = 31
N_XATTN_HEADS = 4
XATTN_HEAD_DIM = 256
D_FF = 2816
D_IN = 3 * D_ATTN + N_ATTN_HEADS + 2 * D_CONV
EPS = 1e-6
NEG_INF = -1e30
LOG2E = math.log2(math.e)

LANES = 128
SUBLANES = 8
MXU_DIM = 256
VMEM_LIMIT = 56 * 1024 * 1024

TOK_TILE = 512
ATT_TILE = 512
KEY_TILE = ATT_TILE
DIAG_BAND = 256
FF_CHUNK = MXU_DIM
CONV_HALO = 32
CONV_ROWS = 64
OUT_TILE = 1024
MEM_BATCH = 4
N_PAIRS = D_ATTN // LANES


def _rms(x, g):
    return x * lax.rsqrt(jnp.mean(x * x, axis=-1, keepdims=True) + EPS) * g


def _const_spec(shape):
    nd = len(shape)
    return pl.BlockSpec(shape, lambda *_: (0,) * nd, pipeline_mode=pl.Buffered(1))


def _layer_spec(shape, layer):
    nd = len(shape)
    return pl.BlockSpec((1,) + shape, lambda *_: (layer,) + (0,) * nd, pipeline_mode=pl.Buffered(1))


def _ffn_kernel(x_ref, xn_ref, g_ref, wg_ref, wu_ref, wd_ref, fg_ref, o_ref,
                h_scr, hn_scr, act_scr, act0_scr, *, final):
    def hidden_chunk(h_ref, c):
        cols = slice(c * FF_CHUNK, (c + 1) * FF_CHUNK)
        h = h_ref[...]
        gate = jnp.dot(h, wg_ref[0, :, cols].astype(BF16), preferred_element_type=F32)
        up = jnp.dot(h, wu_ref[0, :, cols].astype(BF16), preferred_element_type=F32)
        return (gate * jax.nn.sigmoid(gate) * up).astype(BF16)

    @pl.when(pl.program_id(0) == 0)
    def _():
        hn_scr[...] = _rms(x_ref[...], g_ref[...]).astype(BF16)
        act0_scr[...] = hidden_chunk(hn_scr, 0)

    h_scr[...] = hn_scr[...]
    act_scr[:, :FF_CHUNK] = act0_scr[...]
    for c in range(1, D_FF // FF_CHUNK):
        act_scr[:, c * FF_CHUNK:(c + 1) * FF_CHUNK] = hidden_chunk(h_scr, c)
    hn_scr[...] = _rms(xn_ref[...], g_ref[...]).astype(BF16)
    y = jnp.dot(act_scr[...], wd_ref[0].astype(BF16), preferred_element_type=F32)
    act0_scr[...] = hidden_chunk(hn_scr, 0)
    out = x_ref[...] + 0.5 * y
    if final:
        out = _rms(out, fg_ref[...])
    o_ref[...] = out


def _ffn(x2d, g, wg, wu, wd, fg, layer, *, final):
    t = x2d.shape[0]
    n_tiles = t // TOK_TILE
    tile = pl.BlockSpec((TOK_TILE, D_MODEL), lambda i: (i, 0))
    next_tile = pl.BlockSpec((TOK_TILE, D_MODEL), lambda i: (jnp.minimum(i + 1, n_tiles - 1), 0))
    return pl.pallas_call(
        functools.partial(_ffn_kernel, final=final),
        out_shape=jax.ShapeDtypeStruct((t, D_MODEL), F32),
        grid=(n_tiles,),
        in_specs=[tile, next_tile, _const_spec((1, D_MODEL)), _layer_spec((D_MODEL, D_FF), layer),
                  _layer_spec((D_MODEL, D_FF), layer), _layer_spec((D_FF, D_MODEL), layer),
                  _const_spec((1, D_MODEL))],
        out_specs=tile,
        scratch_shapes=[pltpu.VMEM((TOK_TILE, D_MODEL), BF16), pltpu.VMEM((TOK_TILE, D_MODEL), BF16),
                        pltpu.VMEM((TOK_TILE, D_FF), BF16), pltpu.VMEM((TOK_TILE, FF_CHUNK), BF16)],
        compiler_params=pltpu.CompilerParams(
            dimension_semantics=("arbitrary",), vmem_limit_bytes=VMEM_LIMIT),
        name="ffn_final" if final else "ffn",
    )(x2d, x2d, g, wg, wu, wd, fg)


def _memkv_kernel(mem_ref, g_ref, w_ref, k_ref, v_ref):
    mem = mem_ref[...].reshape(MEM_BATCH * N_MEM, D_MODEL)
    mn = _rms(mem, g_ref[0]).astype(BF16)
    kv = jnp.dot(mn, w_ref[0].astype(BF16), preferred_element_type=F32)
    k_ref[0] = kv[:, :D_MODEL].astype(BF16).reshape(MEM_BATCH, N_MEM, D_MODEL)
    v_ref[0] = kv[:, D_MODEL:].astype(BF16).reshape(MEM_BATCH, N_MEM, D_MODEL)


def _memkv(mem, g, w_kv):
    depth, batch = g.shape[0], mem.shape[0]
    out = jax.ShapeDtypeStruct((depth, batch, N_MEM, D_MODEL), BF16)
    out_spec = pl.BlockSpec((1, MEM_BATCH, N_MEM, D_MODEL), lambda l, b: (l, b, 0, 0))
    return pl.pallas_call(
        _memkv_kernel,
        out_shape=(out, out),
        grid=(depth, batch // MEM_BATCH),
        in_specs=[pl.BlockSpec((MEM_BATCH, N_MEM, D_MODEL), lambda l, b: (b, 0, 0)),
                  pl.BlockSpec((1, 1, D_MODEL), lambda l, b: (l, 0, 0)),
                  pl.BlockSpec((1, D_MODEL, 2 * D_MODEL), lambda l, b: (l, 0, 0))],
        out_specs=(out_spec, out_spec),
        compiler_params=pltpu.CompilerParams(
            dimension_semantics=("arbitrary", "arbitrary"), vmem_limit_bytes=VMEM_LIMIT),
        name="mem_kv",
    )(mem, g, w_kv)


def _log_sigmoid(z):
    return jnp.minimum(z, 0.0) - jnp.log1p(jnp.exp(-jnp.abs(z)))


def _split3(a):
    hi = a.astype(BF16).astype(F32)
    r = a - hi
    mid = r.astype(BF16).astype(F32)
    lo = (r - mid).astype(BF16).astype(F32)
    return hi, mid, lo


def _mix_in_kernel(x_ref, g_ref, w_ref, bf_ref, tri_ref, cw_ref, cb_ref, lng_ref,
                   lnb_ref, cog_ref, q_ref, k_ref, v_ref, c_ref, u_ref,
                   wqkv_scr, wf_scr, wag_scr, h_scr, ubuf, shifted, carry):
    s_idx = pl.program_id(1)

    @pl.when(jnp.logical_and(pl.program_id(0) == 0, s_idx == 0))
    def _():
        f0 = 3 * D_ATTN
        wqkv_scr[...] = jnp.transpose(w_ref[0, :f0, :]).astype(BF16)
        wf_scr[...] = jnp.transpose(w_ref[0, f0:f0 + LANES, :]).astype(BF16)
        wag_scr[...] = jnp.transpose(w_ref[0, f0 + N_ATTN_HEADS:, :]).astype(BF16)

    def project(w_scr):
        return jnp.dot(h_scr[...], w_scr[...], preferred_element_type=F32)

    @pl.when(s_idx == 0)
    def _():
        carry[...] = jnp.zeros_like(carry)
        ubuf[0:CONV_HALO, :] = jnp.zeros((CONV_HALO, D_CONV), F32)

    h_scr[...] = _rms(x_ref[0], g_ref[...]).astype(BF16)

    ag = project(wag_scr)
    ubuf[CONV_HALO:, :] = ag[:, :D_CONV] * jax.nn.sigmoid(ag[:, D_CONV:])
    n_shift = TOK_TILE + CONV_HALO - SUBLANES
    for r in range(1, SUBLANES):
        shifted[r - 1] = ubuf[r:r + n_shift, :]

    qkv = project(wqkv_scr)
    q_ref[0] = (qkv[:, :D_ATTN] * (LOG2E / math.sqrt(HEAD_DIM))).astype(BF16)
    k_ref[0] = qkv[:, D_ATTN:2 * D_ATTN].astype(BF16)
    v_ref[0] = qkv[:, 2 * D_ATTN:].astype(BF16)

    f_t = jnp.transpose(project(wf_scr))
    logf = _log_sigmoid(f_t[:N_ATTN_HEADS, :] + bf_ref[...]) * LOG2E
    parts = jnp.concatenate(_split3(logf) + (jnp.zeros_like(logf),), axis=0).astype(BF16)
    sums = jnp.dot(parts, tri_ref[...], preferred_element_type=F32)
    h8 = N_ATTN_HEADS
    c_tile = carry[...] + sums[:h8] + sums[h8:2 * h8] + sums[2 * h8:3 * h8]
    c_ref[0] = c_tile
    carry[...] = c_tile[:, TOK_TILE - 1:TOK_TILE]

    for rb in range(TOK_TILE // CONV_ROWS):
        r0 = rb * CONV_ROWS
        acc = jnp.zeros((CONV_ROWS // SUBLANES, SUBLANES, D_CONV), F32)
        for tap in range(CONV_WIDTH):
            off = r0 + tap + CONV_HALO - (CONV_WIDTH - 1)
            sh, al = off % SUBLANES, (off // SUBLANES) * SUBLANES
            if sh == 0:
                blk = ubuf[al:al + CONV_ROWS, :]
            else:
                blk = shifted[sh - 1, al:al + CONV_ROWS, :]
            acc = acc + blk.reshape(CONV_ROWS // SUBLANES, SUBLANES, D_CONV) * cw_ref[tap]
        y = acc.reshape(CONV_ROWS, D_CONV) + cb_ref[...]
        mu = jnp.mean(y, axis=-1, keepdims=True)
        yc = y - mu
        yn = yc * lax.rsqrt(jnp.mean(yc * yc, axis=-1, keepdims=True) + EPS) * lng_ref[...] + lnb_ref[...]
        sw = yn * jax.nn.sigmoid(yn)
        u_ref[0, r0:r0 + CONV_ROWS, :] = _rms(sw, cog_ref[...]).astype(BF16)

    ubuf[0:CONV_HALO, :] = ubuf[TOK_TILE:TOK_TILE + CONV_HALO, :]


def _mix_in(x, g, w, bf, tri, cw, cb, lng, lnb, cog, layer):
    batch, seq, _ = x.shape
    tok = lambda w: pl.BlockSpec((1, TOK_TILE, w), lambda b, s: (b, s, 0))
    n_shift = TOK_TILE + CONV_HALO - SUBLANES
    outs = (jax.ShapeDtypeStruct((batch, seq, D_ATTN), BF16),) * 3 + (
        jax.ShapeDtypeStruct((batch, N_ATTN_HEADS, seq), F32),
        jax.ShapeDtypeStruct((batch, seq, D_CONV), BF16))
    return pl.pallas_call(
        _mix_in_kernel,
        out_shape=outs,
        grid=(batch, seq // TOK_TILE),
        in_specs=[tok(D_MODEL), _const_spec((1, D_MODEL)), _layer_spec((D_IN, D_MODEL), layer),
                  _const_spec((N_ATTN_HEADS, 1)), _const_spec((TOK_TILE, TOK_TILE)),
                  _const_spec((CONV_WIDTH, SUBLANES, D_CONV)),
                  _const_spec((1, D_CONV)), _const_spec((1, D_CONV)), _const_spec((1, D_CONV)),
                  _const_spec((1, D_CONV))],
        out_specs=(tok(D_ATTN), tok(D_ATTN), tok(D_ATTN),
                   pl.BlockSpec((1, N_ATTN_HEADS, TOK_TILE), lambda b, s: (b, 0, s)), tok(D_CONV)),
        scratch_shapes=[pltpu.VMEM((D_MODEL, 3 * D_ATTN), BF16), pltpu.VMEM((D_MODEL, LANES), BF16),
                        pltpu.VMEM((D_MODEL, 2 * D_CONV), BF16),
                        pltpu.VMEM((TOK_TILE, D_MODEL), BF16),
                        pltpu.VMEM((TOK_TILE + CONV_HALO, D_CONV), F32),
                        pltpu.VMEM((SUBLANES - 1, n_shift, D_CONV), F32),
                        pltpu.VMEM((N_ATTN_HEADS, 1), F32)],
        compiler_params=pltpu.CompilerParams(
            dimension_semantics=("arbitrary", "arbitrary"), vmem_limit_bytes=VMEM_LIMIT),
        name="mix_in",
    )(x, g, w, bf, tri, cw, cb, lng, lnb, cog)


def _fox_kernel(q_ref, k_ref, v_ref, cq_ref, crow_ref, rep_ref, g_ref, o_ref,
                qm_scr, ct_scr, m_scr, acc_scr, s_scr):
    i = pl.program_id(1)
    n_full = i
    lane = lax.broadcasted_iota(jnp.int32, (ATT_TILE, LANES), 1)
    first = lane < HEAD_DIM

    for p in range(N_PAIRS):
        qp = q_ref[0, :, p * LANES:(p + 1) * LANES]
        zero = jnp.zeros_like(qp)
        qm_scr[2 * p] = jnp.where(first, qp, zero)
        qm_scr[2 * p + 1] = jnp.where(first, zero, qp)
    parts = jnp.concatenate(_split3(cq_ref[0]) + (jnp.zeros((N_ATTN_HEADS, ATT_TILE), F32),), axis=0)
    c_t = lax.dot_general(parts.astype(BF16), rep_ref[...], (((0,), (0,)), ((), ())),
                          preferred_element_type=F32)
    for h in range(N_ATTN_HEADS):
        ct_scr[h] = c_t[:, h * LANES:(h + 1) * LANES]

    def scores(t, h, n_chunk):
        k0 = pl.multiple_of(t * KEY_TILE, KEY_TILE)
        pair = slice((h // 2) * LANES, (h // 2 + 1) * LANES)
        c_s = crow_ref[0, t]
        s = lax.dot_general(qm_scr[h], k_ref[0, pl.ds(k0, n_chunk * LANES), pair],
                            (((1,), (1,)), ((), ())), preferred_element_type=F32)
        ct = ct_scr[h]
        for c in range(n_chunk):
            cols = slice(c * LANES, (c + 1) * LANES)
            s_scr[h, :, cols] = s[:, cols] + ct - c_s[h:h + 1, cols]

    def update(t, h, r0, n_rows, n_chunk, diagonal, first_window):
        k0 = pl.multiple_of(t * KEY_TILE, KEY_TILE)
        rows = slice(r0, r0 + n_rows)
        pair = slice((h // 2) * LANES, (h // 2 + 1) * LANES)

        q_pos = lax.broadcasted_iota(jnp.int32, (n_rows, LANES), 0) + r0
        k_pos = lax.broadcasted_iota(jnp.int32, (n_rows, LANES), 1)

        def logits(c):
            sc = s_scr[h, rows, c * LANES:(c + 1) * LANES]
            if diagonal and (c + 1) * LANES > r0:
                sc = jnp.where(q_pos >= k_pos + c * LANES, sc, NEG_INF)
            return sc

        m_blk = functools.reduce(jnp.maximum, [logits(c) for c in range(n_chunk)])
        m_row = jnp.max(m_blk, axis=-1, keepdims=True)
        if first_window:
            m_new = jnp.broadcast_to(m_row, (n_rows, LANES))
        else:
            m_old = m_scr[h, rows, :]
            m_new = jnp.maximum(m_old, m_row)
            alpha = jnp.exp2(m_old - m_new)
            acc_scr[h, rows, :] = jnp.concatenate([alpha, alpha], axis=1) * acc_scr[h, rows, :]
        p_bf = jnp.concatenate([jnp.exp2(logits(c) - m_new) for c in range(n_chunk)], axis=1).astype(BF16)
        m_scr[h, rows, :] = m_new
        v_ones = jnp.concatenate([v_ref[0, pl.ds(k0, n_chunk * LANES), pair],
                                  jnp.ones((n_chunk * LANES, LANES), BF16)], axis=1)
        pv = jnp.dot(p_bf, v_ones, preferred_element_type=F32)
        if first_window:
            acc_scr[h, rows, :] = pv
        else:
            acc_scr[h, rows, :] += pv

    all_chunks = KEY_TILE // LANES
    last = N_ATTN_HEADS - 1

    def full_window(t, first_window):
        for h in range(N_ATTN_HEADS):
            if h < last:
                scores(t, h + 1, all_chunks)
            else:
                scores(t + 1, 0, all_chunks)
            update(t, h, 0, ATT_TILE, all_chunks, False, first_window)

    def diagonal_window(first_window):
        for h in range(N_ATTN_HEADS):
            if h < last:
                scores(n_full, h + 1, all_chunks)
            for r0 in range(0, ATT_TILE, DIAG_BAND):
                update(n_full, h, r0, DIAG_BAND, (r0 + DIAG_BAND) // LANES, True, first_window)

    scores(0, 0, all_chunks)

    @pl.when(i == 0)
    def _():
        diagonal_window(True)

    @pl.when(i > 0)
    def _():
        full_window(0, True)
        lax.fori_loop(1, n_full, lambda t, c: (full_window(t, False), c)[1], 0)
        diagonal_window(False)

    outs = []
    for p in range(N_PAIRS):
        o0 = acc_scr[2 * p, :, :LANES] / acc_scr[2 * p, :, LANES:]
        o1 = acc_scr[2 * p + 1, :, :LANES] / acc_scr[2 * p + 1, :, LANES:]
        outs.append(jnp.where(first, o0, o1))
    o_ref[0] = _rms(jnp.concatenate(outs, axis=1), g_ref[...]).astype(BF16)


def _fox_attention(q, k, v, c, crow, rep, g):
    batch, seq, _ = q.shape
    n_win = seq // KEY_TILE
    stat = pltpu.VMEM((N_ATTN_HEADS, ATT_TILE, LANES), F32)
    return pl.pallas_call(
        _fox_kernel,
        out_shape=jax.ShapeDtypeStruct((batch, seq, D_ATTN), BF16),
        grid=(batch, seq // ATT_TILE),
        in_specs=[pl.BlockSpec((1, ATT_TILE, D_ATTN), lambda b, i: (b, i, 0)),
                  pl.BlockSpec((1, seq, D_ATTN), lambda b, i: (b, 0, 0)),
                  pl.BlockSpec((1, seq, D_ATTN), lambda b, i: (b, 0, 0)),
                  pl.BlockSpec((1, N_ATTN_HEADS, ATT_TILE), lambda b, i: (b, 0, i)),
                  pl.BlockSpec((1, n_win, N_ATTN_HEADS, KEY_TILE), lambda b, i: (b, 0, 0, 0)),
                  _const_spec((4 * N_ATTN_HEADS, N_ATTN_HEADS * LANES)), _const_spec((1, D_ATTN))],
        out_specs=pl.BlockSpec((1, ATT_TILE, D_ATTN), lambda b, i: (b, i, 0)),
        scratch_shapes=[pltpu.VMEM((N_ATTN_HEADS, ATT_TILE, LANES), BF16), stat, stat,
                        pltpu.VMEM((N_ATTN_HEADS, ATT_TILE, 2 * LANES), F32),
                        pltpu.VMEM((N_ATTN_HEADS, ATT_TILE, KEY_TILE), F32)],
        compiler_params=pltpu.CompilerParams(
            dimension_semantics=("arbitrary", "arbitrary"), vmem_limit_bytes=VMEM_LIMIT),
        name="fox_attn",
    )(q, k, v, c, crow, rep, g)


def _mix_out_kernel(x_ref, a_ref, u_ref, wout_ref, xg_ref, wq_ref, mk_ref, mv_ref, wo_ref,
                    o_ref, o_scr):
    mixed = jnp.concatenate([a_ref[0], u_ref[0]], axis=1)
    x = x_ref[0] + jnp.dot(mixed, wout_ref[0].astype(BF16), preferred_element_type=F32)

    hq = _rms(x, xg_ref[...]).astype(BF16)
    q = jnp.dot(hq, wq_ref[0].astype(BF16), preferred_element_type=F32) * (LOG2E / math.sqrt(XATTN_HEAD_DIM))
    q = q.astype(BF16)
    for h in range(N_XATTN_HEADS):
        cols = slice(h * XATTN_HEAD_DIM, (h + 1) * XATTN_HEAD_DIM)
        s = lax.dot_general(q[:, cols], mk_ref[0, 0, :, cols], (((1,), (1,)), ((), ())),
                            preferred_element_type=F32)
        p = jnp.exp2(s - jnp.max(s, axis=-1, keepdims=True))
        pv = jnp.dot(p.astype(BF16), mv_ref[0, 0, :, cols], preferred_element_type=F32)
        o_scr[:, cols] = (pv / jnp.sum(p, axis=-1, keepdims=True)).astype(BF16)
    o_ref[0] = x + jnp.dot(o_scr[...], wo_ref[0].astype(BF16), preferred_element_type=F32)


def _mix_out(x, attn, u, wout, xg, wq, mk, mv, wo, layer):
    batch, seq, _ = x.shape
    tok = lambda w: pl.BlockSpec((1, OUT_TILE, w), lambda b, s: (b, s, 0))
    mem_spec = pl.BlockSpec((1, 1, N_MEM, D_MODEL), lambda b, s: (layer, b, 0, 0))
    return pl.pallas_call(
        _mix_out_kernel,
        out_shape=jax.ShapeDtypeStruct((batch, seq, D_MODEL), F32),
        grid=(batch, seq // OUT_TILE),
        in_specs=[tok(D_MODEL), tok(D_ATTN), tok(D_CONV),
                  _layer_spec((D_MODEL, D_MODEL), layer), _const_spec((1, D_MODEL)),
                  _layer_spec((D_MODEL, D_MODEL), layer), mem_spec, mem_spec,
                  _layer_spec((D_MODEL, D_MODEL), layer)],
        out_specs=tok(D_MODEL),
        scratch_shapes=[pltpu.VMEM((OUT_TILE, D_MODEL), BF16)],
        compiler_params=pltpu.CompilerParams(
            dimension_semantics=("arbitrary", "arbitrary"), vmem_limit_bytes=VMEM_LIMIT),
        name="mix_out",
    )(x, attn, u, wout, xg, wq, mk, mv, wo)


def _gate_replicator():
    mat = np.zeros((4 * N_ATTN_HEADS, N_ATTN_HEADS * LANES), np.float32)
    for h in range(N_ATTN_HEADS):
        for part in range(3):
            mat[part * N_ATTN_HEADS + h, h * LANES:(h + 1) * LANES] = 1.0
    return mat


def kernel(x, mem, ffn1_norm_g, ffn1_w_gate, ffn1_w_up, ffn1_w_down, mix_norm_g, w_in, b_f, conv_w, conv_b, conv_ln_g, conv_ln_b, attn_out_g, conv_out_g, w_out, xattn_norm_g, mem_norm_g, xattn_w_q, xattn_w_kv, xattn_w_o, ffn2_norm_g, ffn2_w_gate, ffn2_w_up, ffn2_w_down, final_norm_g):
    batch, seq, _ = x.shape
    depth = w_in.shape[0]
    n_win = seq // KEY_TILE
    row = lambda v: v.reshape(1, -1)

    mem_k, mem_v = _memkv(mem, mem_norm_g.reshape(depth, 1, D_MODEL), xattn_w_kv)
    fg = row(final_norm_g)
    tri = jnp.triu(jnp.ones((TOK_TILE, TOK_TILE), BF16))
    w_in_t = jnp.swapaxes(w_in, 1, 2)
    rep = jnp.asarray(_gate_replicator(), BF16)

    for l in range(depth):
        x2d = _ffn(x.reshape(batch * seq, D_MODEL), row(ffn1_norm_g[l]), ffn1_w_gate, ffn1_w_up,
                   ffn1_w_down, fg, l, final=False)
        x = x2d.reshape(batch, seq, D_MODEL)

        cw = jnp.broadcast_to(conv_w[l][:, None, :], (CONV_WIDTH, SUBLANES, D_CONV))
        q, k, v, c, u = _mix_in(
            x, row(mix_norm_g[l]), w_in_t, b_f[l].reshape(N_ATTN_HEADS, 1), tri, cw, row(conv_b[l]),
            row(conv_ln_g[l]), row(conv_ln_b[l]), row(conv_out_g[l]), l)

        crow = c.reshape(batch, N_ATTN_HEADS, n_win, KEY_TILE).transpose(0, 2, 1, 3)
        attn = _fox_attention(q, k, v, c, crow, rep, row(attn_out_g[l]))

        x = _mix_out(x, attn, u, w_out, row(xattn_norm_g[l]), xattn_w_q, mem_k, mem_v, xattn_w_o, l)

        last = l == depth - 1
        x2d = _ffn(x.reshape(batch * seq, D_MODEL), row(ffn2_norm_g[l]), ffn2_w_gate, ffn2_w_up,
                   ffn2_w_down, fg, l, final=last)
        x = x2d.reshape(batch, seq, D_MODEL)
    return x
```

```python
import functools
import math

import jax
import jax.numpy as jnp
import numpy as np
from jax import lax
from jax.experimental import pallas as pl
from jax.experimental.pallas import tpu as pltpu

F32 = jnp.float32
BF16 = jnp.bfloat16

D_MODEL = 1024
N_MEM = 256
D_ATTN = 512
D_CONV = 512
HEAD_DIM = 64
N_ATTN_HEADS = 8
CONV_WIDTH = 31
N_XATTN_HEADS = 4
XATTN_HEAD_DIM = 256
D_FF = 2816
D_IN = 3 * D_ATTN + N_ATTN_HEADS + 2 * D_CONV
EPS = 1e-6
NEG_INF = -1e30
LOG2E = math.log2(math.e)

LANES = 128
SUBLANES = 8
MXU_DIM = 256
VMEM_LIMIT = 56 * 1024 * 1024

TOK_TILE = 512
ATT_TILE = 512
KEY_TILE = ATT_TILE
DIAG_BAND = 256
FF_CHUNK = MXU_DIM
CONV_HALO = 32
CONV_ROWS = 64
OUT_TILE = 1024
MEM_BATCH = 4
N_PAIRS = D_ATTN // LANES


def _rms(x, g):
    return x * lax.rsqrt(jnp.mean(x * x, axis=-1, keepdims=True) + EPS) * g


def _const_spec(shape):
    nd = len(shape)
    return pl.BlockSpec(shape, lambda *_: (0,) * nd, pipeline_mode=pl.Buffered(1))


def _layer_spec(shape, layer):
    nd = len(shape)
    return pl.BlockSpec((1,) + shape, lambda *_: (layer,) + (0,) * nd, pipeline_mode=pl.Buffered(1))


def _ffn_kernel(x_ref, xn_ref, g_ref, wg_ref, wu_ref, wd_ref, fg_ref, o_ref,
                h_scr, hn_scr, act_scr, act0_scr, *, final):
    def hidden_chunk(h_ref, c):
        cols = slice(c * FF_CHUNK, (c + 1) * FF_CHUNK)
        h = h_ref[...]
        gate = jnp.dot(h, wg_ref[0, :, cols].astype(BF16), preferred_element_type=F32)
        up = jnp.dot(h, wu_ref[0, :, cols].astype(BF16), preferred_element_type=F32)
        return (gate * jax.nn.sigmoid(gate) * up).astype(BF16)

    @pl.when(pl.program_id(0) == 0)
    def _():
        hn_scr[...] = _rms(x_ref[...], g_ref[...]).astype(BF16)
        act0_scr[...] = hidden_chunk(hn_scr, 0)

    h_scr[...] = hn_scr[...]
    act_scr[:, :FF_CHUNK] = act0_scr[...]
    for c in range(1, D_FF // FF_CHUNK):
        act_scr[:, c * FF_CHUNK:(c + 1) * FF_CHUNK] = hidden_chunk(h_scr, c)
    hn_scr[...] = _rms(xn_ref[...], g_ref[...]).astype(BF16)
    y = jnp.dot(act_scr[...], wd_ref[0].astype(BF16), preferred_element_type=F32)
    act0_scr[...] = hidden_chunk(hn_scr, 0)
    out = x_ref[...] + 0.5 * y
    if final:
        out = _rms(out, fg_ref[...])
    o_ref[...] = out


def _ffn(x2d, g, wg, wu, wd, fg, layer, *, final):
    t = x2d.shape[0]
    n_tiles = t // TOK_TILE
    tile = pl.BlockSpec((TOK_TILE, D_MODEL), lambda i: (i, 0))
    next_tile = pl.BlockSpec((TOK_TILE, D_MODEL), lambda i: (jnp.minimum(i + 1, n_tiles - 1), 0))
    return pl.pallas_call(
        functools.partial(_ffn_kernel, final=final),
        out_shape=jax.ShapeDtypeStruct((t, D_MODEL), F32),
        grid=(n_tiles,),
        in_specs=[tile, next_tile, _const_spec((1, D_MODEL)), _layer_spec((D_MODEL, D_FF), layer),
                  _layer_spec((D_MODEL, D_FF), layer), _layer_spec((D_FF, D_MODEL), layer),
                  _const_spec((1, D_MODEL))],
        out_specs=tile,
        scratch_shapes=[pltpu.VMEM((TOK_TILE, D_MODEL), BF16), pltpu.VMEM((TOK_TILE, D_MODEL), BF16),
                        pltpu.VMEM((TOK_TILE, D_FF), BF16), pltpu.VMEM((TOK_TILE, FF_CHUNK), BF16)],
        compiler_params=pltpu.CompilerParams(
            dimension_semantics=("arbitrary",), vmem_limit_bytes=VMEM_LIMIT),
        name="ffn_final" if final else "ffn",
    )(x2d, x2d, g, wg, wu, wd, fg)


def _memkv_kernel(mem_ref, g_ref, w_ref, k_ref, v_ref):
    mem = mem_ref[...].reshape(MEM_BATCH * N_MEM, D_MODEL)
    mn = _rms(mem, g_ref[0]).astype(BF16)
    kv = jnp.dot(mn, w_ref[0].astype(BF16), preferred_element_type=F32)
    k_ref[0] = kv[:, :D_MODEL].astype(BF16).reshape(MEM_BATCH, N_MEM, D_MODEL)
    v_ref[0] = kv[:, D_MODEL:].astype(BF16).reshape(MEM_BATCH, N_MEM, D_MODEL)


def _memkv(mem, g, w_kv):
    depth, batch = g.shape[0], mem.shape[0]
    out = jax.ShapeDtypeStruct((depth, batch, N_MEM, D_MODEL), BF16)
    out_spec = pl.BlockSpec((1, MEM_BATCH, N_MEM, D_MODEL), lambda l, b: (l, b, 0, 0))
    return pl.pallas_call(
        _memkv_kernel,
        out_shape=(out, out),
        grid=(depth, batch // MEM_BATCH),
        in_specs=[pl.BlockSpec((MEM_BATCH, N_MEM, D_MODEL), lambda l, b: (b, 0, 0)),
                  pl.BlockSpec((1, 1, D_MODEL), lambda l, b: (l, 0, 0)),
                  pl.BlockSpec((1, D_MODEL, 2 * D_MODEL), lambda l, b: (l, 0, 0))],
        out_specs=(out_spec, out_spec),
        compiler_params=pltpu.CompilerParams(
            dimension_semantics=("arbitrary", "arbitrary"), vmem_limit_bytes=VMEM_LIMIT),
        name="mem_kv",
    )(mem, g, w_kv)


def _log_sigmoid(z):
    return jnp.minimum(z, 0.0) - jnp.log1p(jnp.exp(-jnp.abs(z)))


def _split3(a):
    hi = a.astype(BF16).astype(F32)
    r = a - hi
    mid = r.astype(BF16).astype(F32)
    lo = (r - mid).astype(BF16).astype(F32)
    return hi, mid, lo


def _mix_in_kernel(x_ref, g_ref, w_ref, bf_ref, tri_ref, cw_ref, cb_ref, lng_ref,
                   lnb_ref, cog_ref, q_ref, k_ref, v_ref, c_ref, u_ref,
                   wqkv_scr, wf_scr, wag_scr, h_scr, ubuf, shifted, carry):
    s_idx = pl.program_id(1)

    @pl.when(jnp.logical_and(pl.program_id(0) == 0, s_idx == 0))
    def _():
        f0 = 3 * D_ATTN
        gain = g_ref[...]
        q_rows = lax.broadcasted_iota(jnp.int32, (f0, 1), 0) < D_ATTN
        q_scale = jnp.where(q_rows, LOG2E / math.sqrt(HEAD_DIM), 1.0).astype(F32)
        wqkv_scr[...] = jnp.transpose(w_ref[0, :f0, :] * gain * q_scale).astype(BF16)
        wf_scr[...] = jnp.transpose(w_ref[0, f0:f0 + LANES, :] * gain).astype(BF16)
        wag_scr[...] = jnp.transpose(w_ref[0, f0 + N_ATTN_HEADS:, :] * gain).astype(BF16)

    def project(w_scr):
        return jnp.dot(h_scr[...], w_scr[...], preferred_element_type=F32)

    @pl.when(s_idx == 0)
    def _():
        carry[...] = jnp.zeros_like(carry)
        ubuf[0:CONV_HALO, :] = jnp.zeros((CONV_HALO, D_CONV), F32)

    x = x_ref[0]
    h_scr[...] = (x * lax.rsqrt(jnp.mean(x * x, axis=-1, keepdims=True) + EPS)).astype(BF16)

    ag = project(wag_scr)
    ubuf[CONV_HALO:, :] = ag[:, :D_CONV] * jax.nn.sigmoid(ag[:, D_CONV:])
    n_shift = TOK_TILE + CONV_HALO - SUBLANES
    for r in range(1, SUBLANES):
        shifted[r - 1] = ubuf[r:r + n_shift, :]

    qkv = project(wqkv_scr)
    q_ref[0] = qkv[:, :D_ATTN].astype(BF16)
    k_ref[0] = qkv[:, D_ATTN:2 * D_ATTN].astype(BF16)
    v_ref[0] = qkv[:, 2 * D_ATTN:].astype(BF16)

    f_t = jnp.transpose(project(wf_scr))
    logf = _log_sigmoid(f_t[:N_ATTN_HEADS, :] + bf_ref[...]) * LOG2E
    parts = jnp.concatenate(_split3(logf) + (jnp.zeros_like(logf),), axis=0).astype(BF16)
    sums = jnp.dot(parts, tri_ref[...], preferred_element_type=F32)
    h8 = N_ATTN_HEADS
    c_tile = carry[...] + sums[:h8] + sums[h8:2 * h8] + sums[2 * h8:3 * h8]
    c_ref[0] = c_tile
    carry[...] = c_tile[:, TOK_TILE - 1:TOK_TILE]

    for rb in range(TOK_TILE // CONV_ROWS):
        r0 = rb * CONV_ROWS
        acc = jnp.zeros((CONV_ROWS // SUBLANES, SUBLANES, D_CONV), F32)
        for tap in range(CONV_WIDTH):
            off = r0 + tap + CONV_HALO - (CONV_WIDTH - 1)
            sh, al = off % SUBLANES, (off // SUBLANES) * SUBLANES
            if sh == 0:
                blk = ubuf[al:al + CONV_ROWS, :]
            else:
                blk = shifted[sh - 1, al:al + CONV_ROWS, :]
            acc = acc + blk.reshape(CONV_ROWS // SUBLANES, SUBLANES, D_CONV) * cw_ref[tap]
        y = acc.reshape(CONV_ROWS, D_CONV) + cb_ref[...]
        mu = jnp.mean(y, axis=-1, keepdims=True)
        yc = y - mu
        yn = yc * lax.rsqrt(jnp.mean(yc * yc, axis=-1, keepdims=True) + EPS) * lng_ref[...] + lnb_ref[...]
        sw = yn * jax.nn.sigmoid(yn)
        u_ref[0, r0:r0 + CONV_ROWS, :] = _rms(sw, cog_ref[...]).astype(BF16)

    ubuf[0:CONV_HALO, :] = ubuf[TOK_TILE:TOK_TILE + CONV_HALO, :]


def _mix_in(x, g, w, bf, tri, cw, cb, lng, lnb, cog, layer):
    batch, seq, _ = x.shape
    tok = lambda w: pl.BlockSpec((1, TOK_TILE, w), lambda b, s: (b, s, 0))
    n_shift = TOK_TILE + CONV_HALO - SUBLANES
    outs = (jax.ShapeDtypeStruct((batch, seq, D_ATTN), BF16),) * 3 + (
        jax.ShapeDtypeStruct((batch, N_ATTN_HEADS, seq), F32),
        jax.ShapeDtypeStruct((batch, seq, D_CONV), BF16))
    return pl.pallas_call(
        _mix_in_kernel,
        out_shape=outs,
        grid=(batch, seq // TOK_TILE),
        in_specs=[tok(D_MODEL), _const_spec((1, D_MODEL)), _layer_spec((D_IN, D_MODEL), layer),
                  _const_spec((N_ATTN_HEADS, 1)), _const_spec((TOK_TILE, TOK_TILE)),
                  _const_spec((CONV_WIDTH, SUBLANES, D_CONV)),
                  _const_spec((1, D_CONV)), _const_spec((1, D_CONV)), _const_spec((1, D_CONV)),
                  _const_spec((1, D_CONV))],
        out_specs=(tok(D_ATTN), tok(D_ATTN), tok(D_ATTN),
                   pl.BlockSpec((1, N_ATTN_HEADS, TOK_TILE), lambda b, s: (b, 0, s)), tok(D_CONV)),
        scratch_shapes=[pltpu.VMEM((D_MODEL, 3 * D_ATTN), BF16), pltpu.VMEM((D_MODEL, LANES), BF16),
                        pltpu.VMEM((D_MODEL, 2 * D_CONV), BF16),
                        pltpu.VMEM((TOK_TILE, D_MODEL), BF16),
                        pltpu.VMEM((TOK_TILE + CONV_HALO, D_CONV), F32),
                        pltpu.VMEM((SUBLANES - 1, n_shift, D_CONV), F32),
                        pltpu.VMEM((N_ATTN_HEADS, 1), F32)],
        compiler_params=pltpu.CompilerParams(
            dimension_semantics=("arbitrary", "arbitrary"), vmem_limit_bytes=VMEM_LIMIT),
        name="mix_in",
    )(x, g, w, bf, tri, cw, cb, lng, lnb, cog)


def _fox_kernel(q_ref, k_ref, v_ref, cq_ref, crow_ref, rep_ref, g_ref, o_ref,
                qm_scr, ct_scr, m_scr, acc_scr, s_scr):
    i = pl.program_id(1)
    n_full = i
    lane = lax.broadcasted_iota(jnp.int32, (ATT_TILE, LANES), 1)
    first = lane < HEAD_DIM

    for p in range(N_PAIRS):
        qp = q_ref[0, :, p * LANES:(p + 1) * LANES]
        zero = jnp.zeros_like(qp)
        qm_scr[2 * p] = jnp.where(first, qp, zero)
        qm_scr[2 * p + 1] = jnp.where(first, zero, qp)
    parts = jnp.concatenate(_split3(cq_ref[0]) + (jnp.zeros((N_ATTN_HEADS, ATT_TILE), F32),), axis=0)
    c_t = lax.dot_general(parts.astype(BF16), rep_ref[...], (((0,), (0,)), ((), ())),
                          preferred_element_type=F32)
    for h in range(N_ATTN_HEADS):
        ct_scr[h] = c_t[:, h * LANES:(h + 1) * LANES]

    def scores(t, h, n_chunk):
        k0 = pl.multiple_of(t * KEY_TILE, KEY_TILE)
        pair = slice((h // 2) * LANES, (h // 2 + 1) * LANES)
        c_s = crow_ref[0, t]
        s = lax.dot_general(qm_scr[h], k_ref[0, pl.ds(k0, n_chunk * LANES), pair],
                            (((1,), (1,)), ((), ())), preferred_element_type=F32)
        ct = ct_scr[h]
        for c in range(n_chunk):
            cols = slice(c * LANES, (c + 1) * LANES)
            s_scr[h, :, cols] = s[:, cols] + ct - c_s[h:h + 1, cols]

    def update(t, h, r0, n_rows, n_chunk, diagonal, first_window):
        k0 = pl.multiple_of(t * KEY_TILE, KEY_TILE)
        rows = slice(r0, r0 + n_rows)
        pair = slice((h // 2) * LANES, (h // 2 + 1) * LANES)

        q_pos = lax.broadcasted_iota(jnp.int32, (n_rows, LANES), 0) + r0
        k_pos = lax.broadcasted_iota(jnp.int32, (n_rows, LANES), 1)

        def logits(c):
            sc = s_scr[h, rows, c * LANES:(c + 1) * LANES]
            if diagonal and (c + 1) * LANES > r0:
                sc = jnp.where(q_pos >= k_pos + c * LANES, sc, NEG_INF)
            return sc

        m_blk = functools.reduce(jnp.maximum, [logits(c) for c in range(n_chunk)])
        m_row = jnp.max(m_blk, axis=-1, keepdims=True)
        if first_window:
            m_new = jnp.broadcast_to(m_row, (n_rows, LANES))
        else:
            m_old = m_scr[h, rows, :]
            m_new = jnp.maximum(m_old, m_row)
            alpha = jnp.exp2(m_old - m_new)
            acc_scr[h, rows, :] = jnp.concatenate([alpha, alpha], axis=1) * acc_scr[h, rows, :]
        p_bf = jnp.concatenate([jnp.exp2(logits(c) - m_new) for c in range(n_chunk)], axis=1).astype(BF16)
        m_scr[h, rows, :] = m_new
        v_ones = jnp.concatenate([v_ref[0, pl.ds(k0, n_chunk * LANES), pair],
                                  jnp.ones((n_chunk * LANES, LANES), BF16)], axis=1)
        pv = jnp.dot(p_bf, v_ones, preferred_element_type=F32)
        if first_window:
            acc_scr[h, rows, :] = pv
        else:
            acc_scr[h, rows, :] += pv

    all_chunks = KEY_TILE // LANES
    last = N_ATTN_HEADS - 1

    def full_window(t, first_window):
        for h in range(N_ATTN_HEADS):
            if h < last:
                scores(t, h + 1, all_chunks)
            else:
                scores(t + 1, 0, all_chunks)
            update(t, h, 0, ATT_TILE, all_chunks, False, first_window)

    def diagonal_window(first_window):
        for h in range(N_ATTN_HEADS):
            if h < last:
                scores(n_full, h + 1, all_chunks)
            for r0 in range(0, ATT_TILE, DIAG_BAND):
                update(n_full, h, r0, DIAG_BAND, (r0 + DIAG_BAND) // LANES, True, first_window)

    scores(0, 0, all_chunks)

    @pl.when(i == 0)
    def _():
        diagonal_window(True)

    @pl.when(i > 0)
    def _():
        full_window(0, True)
        lax.fori_loop(1, n_full, lambda t, c: (full_window(t, False), c)[1], 0)
        diagonal_window(False)

    outs = []
    for p in range(N_PAIRS):
        o0 = acc_scr[2 * p, :, :LANES] / acc_scr[2 * p, :, LANES:]
        o1 = acc_scr[2 * p + 1, :, :LANES] / acc_scr[2 * p + 1, :, LANES:]
        outs.append(jnp.where(first, o0, o1))
    o_ref[0] = _rms(jnp.concatenate(outs, axis=1), g_ref[...]).astype(BF16)


def _fox_attention(q, k, v, c, crow, rep, g):
    batch, seq, _ = q.shape
    n_win = seq // KEY_TILE
    stat = pltpu.VMEM((N_ATTN_HEADS, ATT_TILE, LANES), F32)
    return pl.pallas_call(
        _fox_kernel,
        out_shape=jax.ShapeDtypeStruct((batch, seq, D_ATTN), BF16),
        grid=(batch, seq // ATT_TILE),
        in_specs=[pl.BlockSpec((1, ATT_TILE, D_ATTN), lambda b, i: (b, i, 0)),
                  pl.BlockSpec((1, seq, D_ATTN), lambda b, i: (b, 0, 0)),
                  pl.BlockSpec((1, seq, D_ATTN), lambda b, i: (b, 0, 0)),
                  pl.BlockSpec((1, N_ATTN_HEADS, ATT_TILE), lambda b, i: (b, 0, i)),
                  pl.BlockSpec((1, n_win, N_ATTN_HEADS, KEY_TILE), lambda b, i: (b, 0, 0, 0)),
                  _const_spec((4 * N_ATTN_HEADS, N_ATTN_HEADS * LANES)), _const_spec((1, D_ATTN))],
        out_specs=pl.BlockSpec((1, ATT_TILE, D_ATTN), lambda b, i: (b, i, 0)),
        scratch_shapes=[pltpu.VMEM((N_ATTN_HEADS, ATT_TILE, LANES), BF16), stat, stat,
                        pltpu.VMEM((N_ATTN_HEADS, ATT_TILE, 2 * LANES), F32),
                        pltpu.VMEM((N_ATTN_HEADS, ATT_TILE, KEY_TILE), F32)],
        compiler_params=pltpu.CompilerParams(
            dimension_semantics=("arbitrary", "arbitrary"), vmem_limit_bytes=VMEM_LIMIT),
        name="fox_attn",
    )(q, k, v, c, crow, rep, g)


def _mix_out_kernel(x_ref, a_ref, u_ref, wout_ref, xg_ref, wq_ref, mk_ref, mv_ref, wo_ref,
                    o_ref, o_scr):
    mixed = jnp.concatenate([a_ref[0], u_ref[0]], axis=1)
    x = x_ref[0] + jnp.dot(mixed, wout_ref[0].astype(BF16), preferred_element_type=F32)

    hq = _rms(x, xg_ref[...]).astype(BF16)
    q = jnp.dot(hq, wq_ref[0].astype(BF16), preferred_element_type=F32) * (LOG2E / math.sqrt(XATTN_HEAD_DIM))
    q = q.astype(BF16)
    for h in range(N_XATTN_HEADS):
        cols = slice(h * XATTN_HEAD_DIM, (h + 1) * XATTN_HEAD_DIM)
        s = lax.dot_general(q[:, cols], mk_ref[0, 0, :, cols], (((1,), (1,)), ((), ())),
                            preferred_element_type=F32)
        p = jnp.exp2(s - jnp.max(s, axis=-1, keepdims=True))
        pv = jnp.dot(p.astype(BF16), mv_ref[0, 0, :, cols], preferred_element_type=F32)
        o_scr[:, cols] = (pv / jnp.sum(p, axis=-1, keepdims=True)).astype(BF16)
    o_ref[0] = x + jnp.dot(o_scr[...], wo_ref[0].astype(BF16), preferred_element_type=F32)


def _mix_out(x, attn, u, wout, xg, wq, mk, mv, wo, layer):
    batch, seq, _ = x.shape
    tok = lambda w: pl.BlockSpec((1, OUT_TILE, w), lambda b, s: (b, s, 0))
    mem_spec = pl.BlockSpec((1, 1, N_MEM, D_MODEL), lambda b, s: (layer, b, 0, 0))
    return pl.pallas_call(
        _mix_out_kernel,
        out_shape=jax.ShapeDtypeStruct((batch, seq, D_MODEL), F32),
        grid=(batch, seq // OUT_TILE),
        in_specs=[tok(D_MODEL), tok(D_ATTN), tok(D_CONV),
                  _layer_spec((D_MODEL, D_MODEL), layer), _const_spec((1, D_MODEL)),
                  _layer_spec((D_MODEL, D_MODEL), layer), mem_spec, mem_spec,
                  _layer_spec((D_MODEL, D_MODEL), layer)],
        out_specs=tok(D_MODEL),
        scratch_shapes=[pltpu.VMEM((OUT_TILE, D_MODEL), BF16)],
        compiler_params=pltpu.CompilerParams(
            dimension_semantics=("arbitrary", "arbitrary"), vmem_limit_bytes=VMEM_LIMIT),
        name="mix_out",
    )(x, attn, u, wout, xg, wq, mk, mv, wo)


def _gate_replicator():
    mat = np.zeros((4 * N_ATTN_HEADS, N_ATTN_HEADS * LANES), np.float32)
    for h in range(N_ATTN_HEADS):
        for part in range(3):
            mat[part * N_ATTN_HEADS + h, h * LANES:(h + 1) * LANES] = 1.0
    return mat


def kernel(x, mem, ffn1_norm_g, ffn1_w_gate, ffn1_w_up, ffn1_w_down, mix_norm_g, w_in, b_f, conv_w, conv_b, conv_ln_g, conv_ln_b, attn_out_g, conv_out_g, w_out, xattn_norm_g, mem_norm_g, xattn_w_q, xattn_w_kv, xattn_w_o, ffn2_norm_g, ffn2_w_gate, ffn2_w_up, ffn2_w_down, final_norm_g):
    batch, seq, _ = x.shape
    depth = w_in.shape[0]
    n_win = seq // KEY_TILE
    row = lambda v: v.reshape(1, -1)

    mem_k, mem_v = _memkv(mem, mem_norm_g.reshape(depth, 1, D_MODEL), xattn_w_kv)
    fg = row(final_norm_g)
    tri = jnp.triu(jnp.ones((TOK_TILE, TOK_TILE), BF16))
    w_in_t = jnp.swapaxes(w_in, 1, 2)
    rep = jnp.asarray(_gate_replicator(), BF16)

    for l in range(depth):
        x2d = _ffn(x.reshape(batch * seq, D_MODEL), row(ffn1_norm_g[l]), ffn1_w_gate, ffn1_w_up,
                   ffn1_w_down, fg, l, final=False)
        x = x2d.reshape(batch, seq, D_MODEL)

        cw = jnp.broadcast_to(conv_w[l][:, None, :], (CONV_WIDTH, SUBLANES, D_CONV))
        q, k, v, c, u = _mix_in(
            x, row(mix_norm_g[l]), w_in_t, b_f[l].reshape(N_ATTN_HEADS, 1), tri, cw, row(conv_b[l]),
            row(conv_ln_g[l]), row(conv_ln_b[l]), row(conv_out_g[l]), l)

        crow = c.reshape(batch, N_ATTN_HEADS, n_win, KEY_TILE).transpose(0, 2, 1, 3)
        attn = _fox_attention(q, k, v, c, crow, rep, row(attn_out_g[l]))

        x = _mix_out(x, attn, u, w_out, row(xattn_norm_g[l]), xattn_w_q, mem_k, mem_v, xattn_w_o, l)

        last = l == depth - 1
        x2d = _ffn(x.reshape(batch * seq, D_MODEL), row(ffn2_norm_g[l]), ffn2_w_gate, ffn2_w_up,
                   ffn2_w_down, fg, l, final=last)
        x = x2d.reshape(batch, seq, D_MODEL)
    return x
```
